```python
import jax, jax.numpy as jnp
from jax import lax
import numpy as np

D_MODEL = 1024
BATCH = 8
SEQ = 8192
DEPTH = 1
DEC_BATCH = 128
DEC_SEQ = 1
PAST_LEN = 8192
PAGE_SIZE = 128

D_MIX = D_MODEL
GLA_HEADS = 4
GLA_DV = (D_MIX // 2) // GLA_HEADS
GLA_DK = GLA_DV // 2
GLA_QK = GLA_HEADS * GLA_DK
GLA_V = GLA_HEADS * GLA_DV
GLA_GATE_RANK = 16
GLA_GATE_NORM = 16.0
GLA_CHUNK = 64
MOBA_HEADS = 8
MOBA_DH = (D_MIX // 2) // MOBA_HEADS
MOBA_W = MOBA_HEADS * MOBA_DH
MOBA_BLOCK = 256
MOBA_TOPK = 3
MOBA_QCHUNK = 128
ROPE_THETA = 10000.0
LN_EPS = 1e-5
RMS_EPS = 1e-6
DEEPNORM_ALPHA = (2.0 * DEPTH) ** 0.25
DEEPNORM_BETA = (8.0 * DEPTH) ** -0.25
_IN_SPLITS = (GLA_QK, GLA_QK, GLA_V, GLA_V, GLA_GATE_RANK, MOBA_W, MOBA_W, MOBA_W, MOBA_W)
D_IN = GLA_QK * 2 + GLA_V * 2 + GLA_GATE_RANK + MOBA_W * 4

kernel_name = 'hymba_gla_moba_deepnorm_step'


def _rope(x, pos):
    half = x.shape[-1] // 2
    inv = ROPE_THETA ** (-jnp.arange(half, dtype=jnp.float32) / half)
    ang = pos.astype(jnp.float32)[:, None] * inv[None, :]
    cos = jnp.cos(ang)[None, :, None, :]
    sin = jnp.sin(ang)[None, :, None, :]
    x1 = x[..., :half].astype(jnp.float32)
    x2 = x[..., half:].astype(jnp.float32)
    return jnp.concatenate([x1 * cos - x2 * sin, x2 * cos + x1 * sin], axis=-1).astype(x.dtype)


def _layer_norm(z, g, b):
    z32 = z.astype(jnp.float32)
    mu = jnp.mean(z32, axis=-1, keepdims=True)
    var = jnp.mean(jnp.square(z32 - mu), axis=-1, keepdims=True)
    return ((z32 - mu) * lax.rsqrt(var + LN_EPS) * g + b).astype(z.dtype)


def _project(x, pos, w_in):
    n, t, _ = x.shape
    u = jnp.einsum('btd,de->bte', x, w_in)
    offs = np.cumsum(_IN_SPLITS)[:-1].tolist()
    gq, gk, gv, gg, glow, mq, mk, mv, mg = jnp.split(u, offs, axis=-1)
    mq = _rope(mq.reshape(n, t, MOBA_HEADS, MOBA_DH), pos) * (MOBA_DH ** -0.5)
    mk = _rope(mk.reshape(n, t, MOBA_HEADS, MOBA_DH), pos)
    mv = mv.reshape(n, t, MOBA_HEADS, MOBA_DH)
    return (gq, gk, gv, gg, glow), (mq, mk, mv, mg)


def _gla_recurrence(q, k, v, gk, s0):
    n, t, h, _ = q.shape
    dv = v.shape[-1]
    c = min(GLA_CHUNK, t)
    n_c = -(-t // c)
    pad = n_c * c - t

    def prep(a):
        a = jnp.pad(a.astype(jnp.float32), ((0, 0), (0, pad), (0, 0), (0, 0)))
        return a.reshape(n, n_c, c, h, a.shape[-1]).transpose(1, 0, 3, 2, 4)

    qs, ks, vs, gs = prep(q), prep(k), prep(v), prep(gk)
    causal = jnp.tril(jnp.ones((c, c), dtype=bool))

    def step(s, xs):
        qc, kc, vc, gc = xs
        b = jnp.cumsum(gc, axis=2)
        qd = qc * jnp.exp(b)
        kd = kc * jnp.exp(-b)
        a = jnp.where(causal, jnp.einsum('nhcd,nhsd->nhcs', qd, kd), 0.0)
        o = jnp.einsum('nhcd,nhde->nhce', qd, s) + jnp.einsum('nhcs,nhse->nhce', a, vc)
        b_last = b[:, :, -1:, :]
        s = jnp.exp(b_last[:, :, 0, :])[..., None] * s + jnp.einsum('nhcd,nhce->nhde', kc * jnp.exp(b_last - b), vc)
        return s, o

    s_fin, o = lax.scan(step, s0.astype(jnp.float32), (qs, ks, vs, gs))
    o = o.transpose(1, 0, 3, 2, 4).reshape(n, n_c * c, h, dv)[:, :t]
    return o, s_fin


def _gla_branch(gq, gk, gv, gg, glow, w_gk2, b_gk, gla_norm_w, s0):
    n, t, _ = gq.shape
    q = gq.reshape(n, t, GLA_HEADS, GLA_DK) * (GLA_DK ** -0.5)
    k = gk.reshape(n, t, GLA_HEADS, GLA_DK)
    v = gv.reshape(n, t, GLA_HEADS, GLA_DV)
    logf = jax.nn.log_sigmoid((jnp.einsum('btr,re->bte', glow, w_gk2) + b_gk).astype(jnp.float32)) / GLA_GATE_NORM
    logf = logf.reshape(n, t, GLA_HEADS, GLA_DK)
    o, s_fin = _gla_recurrence(q, k, v, logf, s0)
    o = o * lax.rsqrt(jnp.mean(jnp.square(o), axis=-1, keepdims=True) + RMS_EPS) * gla_norm_w
    o = o.reshape(n, t, GLA_V).astype(gq.dtype) * jax.nn.silu(gg)
    return o, s_fin.astype(s0.dtype)


def _moba_core(q, q_pos, kmean, fetch):
    n, h, t, _ = q.shape
    nb = kmean.shape[2]
    own = q_pos // MOBA_BLOCK
    gate = jnp.einsum('nhtd,nhbd->nhtb', q.astype(jnp.float32), kmean)
    past = jnp.arange(nb)[None, :] < own[:, None]
    gate = jnp.where(past, gate, -jnp.inf)
    _, top = lax.top_k(gate, MOBA_TOPK)
    ok = jnp.arange(MOBA_TOPK)[None, :] < own[:, None]
    blocks = jnp.concatenate([top, jnp.broadcast_to(own[None, None, :, None], (n, h, t, 1))], axis=-1)
    blk_ok = jnp.concatenate([ok, jnp.ones((t, 1), dtype=bool)], axis=-1)
    k1 = MOBA_TOPK + 1
    pos = blocks[..., None] * MOBA_BLOCK + jnp.arange(MOBA_BLOCK)
    mask = jnp.broadcast_to(blk_ok[:, :, None], (n, h, t, k1, MOBA_BLOCK)) & (pos <= q_pos[:, None, None])
    pos = pos.reshape(n, h, t, k1 * MOBA_BLOCK)
    mask = mask.reshape(n, h, t, k1 * MOBA_BLOCK)
    kg, vg = fetch(pos)
    s = jnp.einsum('nhtd,nhtrd->nhtr', q, kg).astype(jnp.float32)
    p = jax.nn.softmax(jnp.where(mask, s, -jnp.inf), axis=-1)
    return jnp.einsum('nhtr,nhtrd->nhtd', p.astype(vg.dtype), vg)


def _moba_prompt(q, k, v):
    b_, s_, h, dh = k.shape
    nb = max(-(-s_ // MOBA_BLOCK), MOBA_TOPK)
    kp = jnp.pad(k.astype(jnp.float32), ((0, 0), (0, nb * MOBA_BLOCK - s_), (0, 0), (0, 0)))
    kmean = kp.reshape(b_, nb, MOBA_BLOCK, h, dh).mean(axis=2).transpose(0, 2, 1, 3)
    n_q = s_ // MOBA_QCHUNK
    qi = q.reshape(b_, n_q, MOBA_QCHUNK, h, dh).transpose(0, 1, 3, 2, 4).reshape(b_ * n_q, h, MOBA_QCHUNK, dh)
    bi = jnp.repeat(jnp.arange(b_, dtype=jnp.int32), n_q)
    ci = jnp.broadcast_to(jnp.arange(n_q, dtype=jnp.int32)[None, :], (b_, n_q)).reshape(-1)
    h_idx = jnp.arange(h)[:, None, None]

    def one(args):
        qc, b, c = args
        kb = k[b]
        vb = v[b]

        def fetch(pos):
            pc = jnp.clip(pos[0], 0, s_ - 1)
            return kb[pc, h_idx][None], vb[pc, h_idx][None]

        q_pos = c * MOBA_QCHUNK + jnp.arange(MOBA_QCHUNK, dtype=jnp.int32)
        return _moba_core(qc[None], q_pos, kmean[b][None], fetch)[0]

    o = lax.map(one, (qi, bi, ci))
    return o.reshape(b_, n_q, h, MOBA_QCHUNK, dh).transpose(0, 1, 3, 2, 4).reshape(b_, s_, h * dh)


def _moba_sample(q, k_new, v_new, cache_k, cache_v, page_table):
    db, t, h, dh = k_new.shape
    n_pages = page_table.shape[1]
    past_len = n_pages * PAGE_SIZE
    total = past_len + t
    nb = max(-(-total // MOBA_BLOCK), MOBA_TOPK)
    page_sums = jnp.sum(cache_k.astype(jnp.float32), axis=1)
    seq_sums = page_sums[page_table]
    page_blk = (jnp.arange(n_pages) * PAGE_SIZE) // MOBA_BLOCK
    page_oh = (page_blk[:, None] == jnp.arange(nb)[None, :]).astype(jnp.float32)
    new_blk = (past_len + jnp.arange(t)) // MOBA_BLOCK
    new_oh = (new_blk[:, None] == jnp.arange(nb)[None, :]).astype(jnp.float32)
    kmean = (jnp.einsum('nphd,pb->nhbd', seq_sums, page_oh)
             + jnp.einsum('nthd,tb->nhbd', k_new.astype(jnp.float32), new_oh)) / MOBA_BLOCK
    h_idx = jnp.arange(h)[None, :, None, None]
    n_idx = jnp.arange(db)[:, None, None, None]

    def fetch(pos):
        in_past = (pos < past_len)[..., None]
        pp = jnp.clip(pos, 0, past_len - 1)
        phys = page_table[n_idx, pp // PAGE_SIZE]
        off = pp % PAGE_SIZE
        npos = jnp.clip(pos - past_len, 0, t - 1)
        kg = jnp.where(in_past, cache_k[phys, off, h_idx], k_new[n_idx, npos, h_idx])
        vg = jnp.where(in_past, cache_v[phys, off, h_idx], v_new[n_idx, npos, h_idx])
        return kg, vg

    q_pos = past_len + jnp.arange(t, dtype=jnp.int32)
    o = _moba_core(q.transpose(0, 2, 1, 3), q_pos, kmean, fetch)
    return o.transpose(0, 2, 1, 3).reshape(db, t, h * dh)


def _merge(x, gla_o, moba_o, mg, w_out, ln_g, ln_b):
    hcat = jnp.concatenate([gla_o, moba_o * jax.nn.silu(mg)], axis=-1)
    return _layer_norm(DEEPNORM_ALPHA * x + jnp.einsum('bte,ed->btd', hcat, w_out), ln_g, ln_b)


def setup_inputs(seed: int = 0) -> dict:
    key = jax.random.key(seed)
    ks = jax.random.split(key, 14)
    n_pages = PAST_LEN // PAGE_SIZE
    n_used = DEC_BATCH * n_pages
    n_phys = n_used + n_used // 4
    f32 = jnp.float32
    x_prompt = jax.random.normal(ks[0], (BATCH, SEQ, D_MODEL), f32)
    x_sample = jax.random.normal(ks[1], (DEC_BATCH, DEC_SEQ, D_MODEL), f32)
    cache_k = jax.random.normal(ks[2], (DEPTH, n_phys, PAGE_SIZE, MOBA_HEADS, MOBA_DH), f32)
    cache_v = jax.random.normal(ks[3], (DEPTH, n_phys, PAGE_SIZE, MOBA_HEADS, MOBA_DH), f32)
    state_gla = jax.random.normal(ks[4], (DEPTH, DEC_BATCH, GLA_HEADS, GLA_DK, GLA_DV), f32)
    page_table = jax.random.permutation(ks[5], n_phys)[:n_used].reshape(DEC_BATCH, n_pages).astype(jnp.int32)
    w_in = jax.random.normal(ks[6], (DEPTH, D_MODEL, D_IN), f32) * (D_MODEL ** -0.5)
    w_gk2 = jax.random.normal(ks[7], (DEPTH, GLA_GATE_RANK, GLA_QK), f32) * (GLA_GATE_RANK ** -0.5)
    b_gk = 0.01 * jax.random.normal(ks[8], (DEPTH, GLA_QK), f32)
    gla_norm_w = 1.0 + 0.01 * jax.random.normal(ks[9], (DEPTH, GLA_DV), f32)
    w_out = jax.random.normal(ks[10], (DEPTH, D_MIX, D_MODEL), f32) * (D_MIX ** -0.5) * DEEPNORM_BETA
    ln_g = 1.0 + 0.01 * jax.random.normal(ks[11], (DEPTH, D_MODEL), f32)
    ln_b = 0.01 * jax.random.normal(ks[12], (DEPTH, D_MODEL), f32)
    return {'x_prompt': x_prompt, 'x_sample': x_sample, 'cache_k': cache_k, 'cache_v': cache_v,
            'state_gla': state_gla, 'page_table': page_table, 'w_in': w_in, 'w_gk2': w_gk2,
            'b_gk': b_gk, 'gla_norm_w': gla_norm_w, 'w_out': w_out, 'ln_g': ln_g, 'ln_b': ln_b}


def reference(x_prompt, x_sample, cache_k, cache_v, state_gla, page_table, w_in, w_gk2, b_gk,
              gla_norm_w, w_out, ln_g, ln_b):
    b_, s_, _ = x_prompt.shape
    pos_p = jnp.arange(s_, dtype=jnp.int32)
    pos_s = page_table.shape[1] * PAGE_SIZE + jnp.arange(x_sample.shape[1], dtype=jnp.int32)
    xp, xs = x_prompt, x_sample
    kp_l, vp_l, sp_l, ks_l, vs_l, ss_l = [], [], [], [], [], []
    for l in range(DEPTH):
        gp, mp = _project(xp, pos_p, w_in[l])
        s0 = jnp.zeros((b_, GLA_HEADS, GLA_DK, GLA_DV), xp.dtype)
        gla_p, st_p = _gla_branch(*gp, w_gk2[l], b_gk[l], gla_norm_w[l], s0)
        moba_p = _moba_prompt(mp[0], mp[1], mp[2])
        kp_l.append(mp[1])
        vp_l.append(mp[2])
        sp_l.append(st_p)
        gs, ms = _project(xs, pos_s, w_in[l])
        gla_s, st_s = _gla_branch(*gs, w_gk2[l], b_gk[l], gla_norm_w[l], state_gla[l])
        moba_s = _moba_sample(ms[0], ms[1], ms[2], cache_k[l], cache_v[l], page_table)
        ks_l.append(ms[1])
        vs_l.append(ms[2])
        ss_l.append(st_s)
        xp = _merge(xp, gla_p, moba_p, mp[3], w_out[l], ln_g[l], ln_b[l])
        xs = _merge(xs, gla_s, moba_s, ms[3], w_out[l], ln_g[l], ln_b[l])
    return (xp, xs, jnp.stack(kp_l), jnp.stack(vp_l), jnp.stack(sp_l),
            jnp.stack(ks_l), jnp.stack(vs_l), jnp.stack(ss_l))
```

```python
import functools
import math

import jax
import jax.numpy as jnp
from jax import lax
from jax.experimental import pallas as pl
from jax.experimental.pallas import tpu as pltpu

GLA_HEADS = 4
GLA_DK = 64
GLA_DV = 128
GLA_QK = GLA_HEADS * GLA_DK
GLA_V = GLA_HEADS * GLA_DV
GLA_GATE_RANK = 16
GLA_GATE_NORM = 16.0
GLA_CHUNK = 64
MOBA_HEADS = 8
MOBA_DH = 64
MOBA_W = MOBA_HEADS * MOBA_DH
MOBA_BLOCK = 256
MOBA_TOPK = 3
ROPE_THETA = 10000.0
LN_EPS = 1e-5
RMS_EPS = 1e-6

LANES = 128
SUBLANES = 8
VMEM_LIMIT_BYTES = 48 * 1024 * 1024

MASK_BIG = 2.0 ** 100
FLAG_LANE0 = MOBA_DH

F32 = jnp.float32
BF16 = jnp.bfloat16
NT_DIMS = (((1,), (1,)), ((), ()))


def _silu(x):
    return x / (1.0 + jnp.exp(-x))


def _log_sigmoid(z):
    return jnp.minimum(z, 0.0) - jnp.log(1.0 + jnp.exp(-jnp.abs(z)))


def _lane_iota(shape):
    return lax.broadcasted_iota(jnp.int32, shape, len(shape) - 1)


def _expand_heads64(x):
    rows, width = x.shape
    low = _lane_iota((1, LANES)) < MOBA_DH
    parts = []
    for g in range(width // LANES):
        a = x[:, g * LANES:(g + 1) * LANES]
        parts.append(jnp.where(low, a, 0.0))
        parts.append(jnp.where(low, pltpu.roll(a, LANES - MOBA_DH, 1), 0.0))
    return jnp.concatenate(parts, axis=1)


def _compact_heads64(slots):
    low = _lane_iota((1, LANES)) < MOBA_DH
    parts = []
    for g in range(len(slots) // 2):
        parts.append(jnp.where(low, slots[2 * g], pltpu.roll(slots[2 * g + 1], MOBA_DH, 1)))
    return jnp.concatenate(parts, axis=1)


def _proj_kernel(x_ref, wg_ref, wl_ref, wgk2_ref, bgk_ref, wm_ref, inv_ref,
                 gq_ref, gk_ref, lf_ref, gv_ref, gg_ref, mq_ref, mk_ref, mv_ref, mg_ref, *ksum_refs,
                 tm, seq_len, pos_base):
    x = x_ref[...].astype(BF16)

    ug = jnp.dot(x, wg_ref[...], preferred_element_type=F32)
    gq_ref[...] = _expand_heads64(ug[:, :GLA_QK] * (GLA_DK ** -0.5))
    gk_ref[...] = _expand_heads64(ug[:, GLA_QK:2 * GLA_QK])
    gv_ref[...] = ug[:, 2 * GLA_QK:2 * GLA_QK + GLA_V]
    gg_ref[...] = ug[:, 2 * GLA_QK + GLA_V:]

    glow = jnp.dot(x, wl_ref[...], preferred_element_type=F32)
    z = jnp.dot(glow.astype(BF16), wgk2_ref[...], preferred_element_type=F32) + bgk_ref[...]
    lf_ref[...] = _expand_heads64(_log_sigmoid(z) / GLA_GATE_NORM)

    um = jnp.dot(x, wm_ref[...], preferred_element_type=F32)
    if seq_len % tm == 0:
        pos0 = pos_base + lax.rem(pl.program_id(0) * tm, seq_len)
        pos = (pos0 + lax.broadcasted_iota(jnp.int32, (tm, 1), 0)).astype(F32)
    else:
        assert seq_len == 1
        pos = jnp.full((tm, 1), pos_base, F32)
    ang = pos * inv_ref[...]
    cos = jnp.cos(ang)
    sin = jnp.sin(ang)
    first_half = (_lane_iota((1, LANES)) % MOBA_DH) < (MOBA_DH // 2)
    sin_signed = jnp.where(first_half, -sin, sin)

    def rope(u):
        parts = []
        for g in range(MOBA_W // LANES):
            a = u[:, g * LANES:(g + 1) * LANES]
            partner = jnp.where(first_half, pltpu.roll(a, LANES - MOBA_DH // 2, 1), pltpu.roll(a, MOBA_DH // 2, 1))
            parts.append(a * cos + partner * sin_signed)
        return jnp.concatenate(parts, axis=1)

    mq_ref[...] = rope(um[:, :MOBA_W]) * (MOBA_DH ** -0.5)
    mk = rope(um[:, MOBA_W:2 * MOBA_W])
    mk_ref[...] = mk
    mv_ref[...] = um[:, 2 * MOBA_W:3 * MOBA_W]
    mg_ref[...] = um[:, 3 * MOBA_W:]
    if ksum_refs:
        ksum_refs[0][...] = jnp.broadcast_to(jnp.sum(mk, axis=0, keepdims=True), (1, SUBLANES, MOBA_W))


def _project(x2d, w_parts, inv_lane, *, tm, seq_len, pos_base, block_sums):
    n_tok, d_model = x2d.shape
    wg, wl, wgk2, bgk, wm = w_parts
    n_tiles = n_tok // tm
    row = lambda i: (i, 0)
    const = lambda i: (0, 0)
    widths = (2 * GLA_QK, 2 * GLA_QK, 2 * GLA_QK, GLA_V, GLA_V, MOBA_W, MOBA_W, MOBA_W, MOBA_W)
    out_shape = [jax.ShapeDtypeStruct((n_tok, w), F32) for w in widths]
    out_specs = [pl.BlockSpec((tm, w), row) for w in widths]
    if block_sums:
        out_shape.append(jax.ShapeDtypeStruct((n_tiles, SUBLANES, MOBA_W), F32))
        out_specs.append(pl.BlockSpec((1, SUBLANES, MOBA_W), lambda i: (i, 0, 0)))
    return pl.pallas_call(
        functools.partial(_proj_kernel, tm=tm, seq_len=seq_len, pos_base=pos_base),
        grid=(n_tiles,),
        in_specs=[pl.BlockSpec((tm, d_model), row),
                  pl.BlockSpec(wg.shape, const), pl.BlockSpec(wl.shape, const), pl.BlockSpec(wgk2.shape, const),
                  pl.BlockSpec(bgk.shape, const), pl.BlockSpec(wm.shape, const), pl.BlockSpec(inv_lane.shape, const)],
        out_specs=out_specs,
        out_shape=out_shape,
        compiler_params=pltpu.CompilerParams(dimension_semantics=("arbitrary",), vmem_limit_bytes=VMEM_LIMIT_BYTES),
        name="proj",
    )(x2d, wg, wl, wgk2, bgk, wm, inv_lane)


def _topk_rows(gate, jrow, n_valid_limit):
    sel = jnp.zeros(gate.shape, F32)
    cur = gate
    for r in range(MOBA_TOPK):
        m = jnp.max(cur, axis=0, keepdims=True)
        idx = jnp.min(jnp.where(cur == m, jrow, 1e9), axis=0, keepdims=True)
        pick = jrow == idx
        counted = jnp.where(r < n_valid_limit, 1.0, 0.0)
        sel = jnp.where(pick, jnp.maximum(sel, counted), sel)
        cur = jnp.where(pick, -jnp.inf, cur)
    return sel


def _moba_prep_kernel(q_ref, k_ref, v_ref, ksum_ref, qe_ref, ke_ref, ve_ref, *, nb):
    i = pl.program_id(1)
    kmean = ksum_ref[0] * (1.0 / MOBA_BLOCK)
    lane = _lane_iota((1, LANES))
    jrow = lax.broadcasted_iota(jnp.int32, (nb, MOBA_BLOCK), 0).astype(F32)
    past = jrow < i.astype(F32)
    q_slots = []
    k_slots = []
    v_slots = []
    km_slots = []
    for ref_or_val, slots in ((q_ref[...], q_slots), (k_ref[...], k_slots), (v_ref[...], v_slots), (kmean, km_slots)):
        e = _expand_heads64(ref_or_val)
        for h in range(MOBA_HEADS):
            slots.append(e[:, h * LANES:(h + 1) * LANES])
    for h in range(MOBA_HEADS):
        gate = lax.dot_general(km_slots[h], q_slots[h], NT_DIMS, precision=lax.Precision.HIGHEST,
                               preferred_element_type=F32)
        sel = _topk_rows(jnp.where(past, gate, -jnp.inf), jrow, i)
        notsel = jnp.where(past, 1.0 - sel, 0.0)
        pre = jnp.concatenate([jnp.zeros((FLAG_LANE0, MOBA_BLOCK), F32), notsel,
                               jnp.zeros((LANES - FLAG_LANE0 - nb, MOBA_BLOCK), F32)], axis=0)
        flags = pre.T
        sl = slice(h * LANES, (h + 1) * LANES)
        qe_ref[:, sl] = (q_slots[h] + flags).astype(BF16)
        ke_ref[:, sl] = jnp.where(lane == FLAG_LANE0 + i, -MASK_BIG, k_slots[h]).astype(BF16)
        ve_ref[:, sl] = jnp.where(lane == FLAG_LANE0, 1.0, v_slots[h]).astype(BF16)


def _moba_prep(mq, mk, mv, ksum, *, batch, nb):
    n_tok = mq.shape[0]
    row = lambda b, i: (b * nb + i, 0)
    ext = jax.ShapeDtypeStruct((n_tok, MOBA_HEADS * LANES), BF16)
    return pl.pallas_call(
        functools.partial(_moba_prep_kernel, nb=nb),
        grid=(batch, nb),
        in_specs=[pl.BlockSpec((MOBA_BLOCK, MOBA_W), row)] * 3 + [pl.BlockSpec((1, nb, MOBA_W), lambda b, i: (b, 0, 0))],
        out_specs=[pl.BlockSpec((MOBA_BLOCK, MOBA_HEADS * LANES), row)] * 3,
        out_shape=[ext] * 3,
        compiler_params=pltpu.CompilerParams(dimension_semantics=("arbitrary", "arbitrary"),
                                             vmem_limit_bytes=VMEM_LIMIT_BYTES),
        name="moba_prep",
    )(mq, mk, mv, ksum)


def _moba_attn_kernel(itab_ref, jtab_ref, q_ref, k_ref, v_ref, o_ref, m_sc, acc_sc):
    p = pl.program_id(1)
    i = itab_ref[p]
    j = jtab_ref[p]

    def step(causal):
        if causal:
            keep = (lax.broadcasted_iota(jnp.int32, (MOBA_BLOCK, MOBA_BLOCK), 1)
                    <= lax.broadcasted_iota(jnp.int32, (MOBA_BLOCK, MOBA_BLOCK), 0))
        for h in range(MOBA_HEADS):
            sl = slice(h * LANES, (h + 1) * LANES)
            s = lax.dot_general(q_ref[:, sl], k_ref[:, sl], NT_DIMS, preferred_element_type=F32)
            if causal:
                s = jnp.where(keep, s, -jnp.inf)
                m_new = jnp.broadcast_to(jnp.max(s, axis=1, keepdims=True), (MOBA_BLOCK, LANES))
            else:
                m_old = m_sc[:, sl]
                m_new = jnp.maximum(m_old, jnp.max(s, axis=1, keepdims=True))
            pexp = jnp.exp(s - m_new[:, :1])
            pv = jnp.dot(pexp.astype(BF16), v_ref[:, sl], preferred_element_type=F32)
            if causal:
                acc_sc[:, sl] = pv
            else:
                acc_sc[:, sl] = jnp.exp(m_old - m_new) * acc_sc[:, sl] + pv
            m_sc[:, sl] = m_new

    @pl.when(j == i)
    def _():
        step(True)

    @pl.when(j != i)
    def _():
        step(False)

    @pl.when((j == i - 1) | (i == 0))
    def _():
        slots = []
        for h in range(MOBA_HEADS):
            acc = acc_sc[:, h * LANES:(h + 1) * LANES]
            slots.append(acc / acc[:, FLAG_LANE0:FLAG_LANE0 + 1])
        o_ref[...] = _compact_heads64(slots)


def _moba_attn(q_ext, k_ext, v_ext, *, batch, nb):
    n_tok = q_ext.shape[0]
    pairs_i = []
    pairs_j = []
    for i in range(nb):
        for j in [i] + list(range(i)):
            pairs_i.append(i)
            pairs_j.append(j)
    itab = jnp.asarray(pairs_i, jnp.int32)
    jtab = jnp.asarray(pairs_j, jnp.int32)
    width = MOBA_HEADS * LANES
    grid_spec = pltpu.PrefetchScalarGridSpec(
        num_scalar_prefetch=2,
        grid=(batch, len(pairs_i)),
        in_specs=[pl.BlockSpec((MOBA_BLOCK, width), lambda b, p, it, jt: (b * nb + it[p], 0)),
                  pl.BlockSpec((MOBA_BLOCK, width), lambda b, p, it, jt: (b * nb + jt[p], 0)),
                  pl.BlockSpec((MOBA_BLOCK, width), lambda b, p, it, jt: (b * nb + jt[p], 0))],
        out_specs=pl.BlockSpec((MOBA_BLOCK, MOBA_W), lambda b, p, it, jt: (b * nb + it[p], 0)),
        scratch_shapes=[pltpu.VMEM((MOBA_BLOCK, width), F32), pltpu.VMEM((MOBA_BLOCK, width), F32)],
    )
    return pl.pallas_call(
        _moba_attn_kernel,
        grid_spec=grid_spec,
        out_shape=jax.ShapeDtypeStruct((n_tok, MOBA_W), F32),
        compiler_params=pltpu.CompilerParams(dimension_semantics=("arbitrary", "arbitrary"),
                                             vmem_limit_bytes=VMEM_LIMIT_BYTES),
        name="moba_attn",
    )(itab, jtab, q_ext, k_ext, v_ext)


def _split3_bf16(x):
    x1 = x.astype(BF16)
    r1 = x - x1.astype(F32)
    x2 = r1.astype(BF16)
    x3 = (r1 - x2.astype(F32)).astype(BF16)
    return x1, x2, x3


def _gla_norm_gate(o, nw, gg):
    on = o * lax.rsqrt(jnp.mean(jnp.square(o), axis=-1, keepdims=True) + RMS_EPS) * nw
    return on * _silu(gg)


def _gla_kernel(q_ref, k_ref, lf_ref, v_ref, g_ref, nw_ref, o_ref, st_ref, st_sc, *, tt):
    t = pl.program_id(1)

    @pl.when(t == 0)
    def _():
        st_sc[...] = jnp.zeros(st_sc.shape, F32)

    c = GLA_CHUNK
    tril = lax.broadcasted_iota(jnp.int32, (c, c), 1) <= lax.broadcasted_iota(jnp.int32, (c, c), 0)
    ltri = jnp.where(tril, 1.0, 0.0).astype(BF16)
    nw = nw_ref[...]
    for ci in range(tt // c):
        rows = slice(ci * c, (ci + 1) * c)
        for h in range(GLA_HEADS):
            sl = slice(h * LANES, (h + 1) * LANES)
            g1, g2, g3 = _split3_bf16(lf_ref[rows, sl])
            b = (jnp.dot(ltri, g1, preferred_element_type=F32) + jnp.dot(ltri, g2, preferred_element_type=F32)
                 + jnp.dot(ltri, g3, preferred_element_type=F32))
            b_last = b[c - 1:c, :]
            q = q_ref[rows, sl]
            k = k_ref[rows, sl]
            v = v_ref[rows, sl]
            qd = (q * jnp.exp(b)).astype(BF16)
            kd = (k * jnp.exp(-b)).astype(BF16)
            kdec = (k * jnp.exp(b_last - b)).astype(BF16)
            a = lax.dot_general(qd, kd, NT_DIMS, preferred_element_type=F32)
            a = jnp.where(tril, a, 0.0)
            st = st_sc[h]
            o = (lax.dot_general(qd, st.astype(BF16), NT_DIMS, preferred_element_type=F32)
                 + jnp.dot(a.astype(BF16), v.astype(BF16), preferred_element_type=F32))
            st_sc[h] = st * jnp.exp(b_last) + jnp.dot(v.T.astype(BF16), kdec, preferred_element_type=F32)
            o_ref[rows, sl] = _gla_norm_gate(o, nw, g_ref[rows, sl]).astype(o_ref.dtype)

    @pl.when(t == pl.num_programs(1) - 1)
    def _():
        for h in range(GLA_HEADS):
            st_ref[0, h] = st_sc[h].T[:GLA_DK, :]


def _gla_prompt(gq, gk, lf, gv, gg, norm_w, *, batch, seq, tt):
    n_tok = gq.shape[0]
    nt = seq // tt
    row = lambda b, t: (b * nt + t, 0)
    pad_w = GLA_HEADS * LANES
    return pl.pallas_call(
        functools.partial(_gla_kernel, tt=tt),
        grid=(batch, nt),
        in_specs=[pl.BlockSpec((tt, pad_w), row)] * 3 + [pl.BlockSpec((tt, GLA_V), row)] * 2
                 + [pl.BlockSpec((1, GLA_DV), lambda b, t: (0, 0))],
        out_specs=[pl.BlockSpec((tt, GLA_V), row),
                   pl.BlockSpec((1, GLA_HEADS, GLA_DK, GLA_DV), lambda b, t: (b, 0, 0, 0))],
        out_shape=[jax.ShapeDtypeStruct((n_tok, GLA_V), BF16),
                   jax.ShapeDtypeStruct((batch, GLA_HEADS, GLA_DK, GLA_DV), F32)],
        scratch_shapes=[pltpu.VMEM((GLA_HEADS, GLA_DV, LANES), F32)],
        compiler_params=pltpu.CompilerParams(dimension_semantics=("arbitrary", "arbitrary"),
                                             vmem_limit_bytes=VMEM_LIMIT_BYTES),
        name="gla_prompt",
    )(gq, gk, lf, gv, gg, norm_w)


def _gla_step_kernel(q_ref, k_ref, lf_ref, v_ref, g_ref, nw_ref, s_ref, o_ref, so_ref):
    eye = (lax.broadcasted_iota(jnp.int32, (GLA_DK, LANES), 0) == lax.broadcasted_iota(jnp.int32, (GLA_DK, LANES), 1))

    def column(rowvec):
        return jnp.sum(jnp.where(eye, rowvec, 0.0), axis=1, keepdims=True)

    nw = nw_ref[...]
    for h in range(GLA_HEADS):
        sl = slice(h * LANES, (h + 1) * LANES)
        decay = column(jnp.exp(lf_ref[0, :, sl]))
        kcol = column(k_ref[0, :, sl])
        qcol = column(q_ref[0, :, sl])
        s_new = decay * s_ref[0, h] + kcol * v_ref[0, :, sl]
        so_ref[0, h] = s_new
        o = jnp.sum(qcol * s_new, axis=0, keepdims=True)
        o_ref[0, :, sl] = _gla_norm_gate(o, nw, g_ref[0, :, sl]).astype(o_ref.dtype)


def _gla_step(gq, gk, lf, gv, gg, norm_w, state):
    n = gq.shape[0]
    pad_w = GLA_HEADS * LANES
    as3 = lambda a: a.reshape(n, 1, a.shape[-1])
    vec = lambda w: pl.BlockSpec((1, 1, w), lambda i: (i, 0, 0))
    st_spec = pl.BlockSpec((1, GLA_HEADS, GLA_DK, GLA_DV), lambda i: (i, 0, 0, 0))
    o, s_out = pl.pallas_call(
        _gla_step_kernel,
        grid=(n,),
        in_specs=[vec(pad_w)] * 3 + [vec(GLA_V)] * 2 + [pl.BlockSpec((1, GLA_DV), lambda i: (0, 0)), st_spec],
        out_specs=[vec(GLA_V), st_spec],
        out_shape=[jax.ShapeDtypeStruct((n, 1, GLA_V), BF16), jax.ShapeDtypeStruct(state.shape, F32)],
        compiler_params=pltpu.CompilerParams(dimension_semantics=("arbitrary",)),
        name="gla_step",
    )(as3(gq), as3(gk), as3(lf), as3(gv), as3(gg), norm_w, state)
    return o.reshape(n, GLA_V), s_out


ROUTE_SLOTS = 4


def _sample_route_kernel(pt_ref, q_ref, ck_hbm, top_ref, buf, sem, km_sc, *, layer, n_pages, page_size, nb_past):
    n = pl.program_id(0)
    pages_per_block = MOBA_BLOCK // page_size
    blocks_in_flight = ROUTE_SLOTS // pages_per_block

    def page_copy(blk, a):
        slot = lax.rem(blk, blocks_in_flight) * pages_per_block + a
        phys = pt_ref[n * n_pages + blk * pages_per_block + a]
        return pltpu.make_async_copy(ck_hbm.at[layer, phys], buf.at[slot], sem.at[slot])

    for blk in range(min(blocks_in_flight, nb_past)):
        for a in range(pages_per_block):
            page_copy(blk, a).start()

    def body(blk, carry):
        total = jnp.zeros((MOBA_HEADS, MOBA_DH), F32)
        for a in range(pages_per_block):
            page_copy(blk, a).wait()
            slot = lax.rem(blk, blocks_in_flight) * pages_per_block + a
            total = total + jnp.sum(buf[slot], axis=0)
        km_sc[blk] = total * (1.0 / MOBA_BLOCK)

        @pl.when(blk + blocks_in_flight < nb_past)
        def _():
            for a in range(pages_per_block):
                page_copy(blk + blocks_in_flight, a).start()
        return carry

    lax.fori_loop(0, nb_past, body, 0)

    gate = jnp.sum(km_sc[...] * q_ref[...], axis=-1, keepdims=True)
    jidx = lax.broadcasted_iota(jnp.int32, gate.shape, 0).astype(F32)
    lane = _lane_iota((MOBA_HEADS, LANES))
    out = jnp.zeros((MOBA_HEADS, LANES), jnp.int32)
    cur = gate
    for r in range(min(MOBA_TOPK, nb_past)):
        m = jnp.max(cur, axis=0, keepdims=True)
        idx = jnp.min(jnp.where(cur == m, jidx, 1e9), axis=0, keepdims=True)
        cur = jnp.where(jidx == idx, -jnp.inf, cur)
        out = jnp.where(lane == r, idx[0].astype(jnp.int32), out)
    top_ref[0] = out


def _sample_route(page_table_flat, q_heads, cache_k, *, layer, nb_past):
    n = q_heads.shape[0]
    page_size = cache_k.shape[2]
    n_pages = page_table_flat.shape[0] // n
    grid_spec = pltpu.PrefetchScalarGridSpec(
        num_scalar_prefetch=1,
        grid=(n,),
        in_specs=[pl.BlockSpec((1, MOBA_HEADS, MOBA_DH), lambda i, pt: (i, 0, 0)),
                  pl.BlockSpec(memory_space=pl.ANY)],
        out_specs=pl.BlockSpec((1, MOBA_HEADS, LANES), lambda i, pt: (i, 0, 0)),
        scratch_shapes=[pltpu.VMEM((ROUTE_SLOTS, page_size, MOBA_HEADS, MOBA_DH), F32),
                        pltpu.SemaphoreType.DMA((ROUTE_SLOTS,)),
                        pltpu.VMEM((nb_past, MOBA_HEADS, MOBA_DH), F32)],
    )
    return pl.pallas_call(
        functools.partial(_sample_route_kernel, layer=layer, n_pages=n_pages, page_size=page_size, nb_past=nb_past),
        grid_spec=grid_spec,
        out_shape=jax.ShapeDtypeStruct((n, MOBA_HEADS, LANES), jnp.int32),
        compiler_params=pltpu.CompilerParams(dimension_semantics=("arbitrary",)),
        name="sample_route",
    )(page_table_flat, q_heads, cache_k)


def _sample_attn_kernel(top_ref, pt_ref, q_ref, kn_ref, vn_ref, ck_hbm, cv_hbm, o_ref, kbuf, vbuf, sem,
                        *, layer, n_pages, page_size, n_sel):
    n = pl.program_id(0)
    pages_per_block = MOBA_BLOCK // page_size

    def copies(h):
        out = []
        for r in range(n_sel):
            blk = top_ref[(n * MOBA_HEADS + h) * n_sel + r]
            for a in range(pages_per_block):
                phys = pt_ref[n * n_pages + blk * pages_per_block + a]
                dst = pl.ds((r * pages_per_block + a) * page_size, page_size)
                out.append(pltpu.make_async_copy(ck_hbm.at[layer, phys, :, h, :], kbuf.at[h, dst, :], sem.at[0, h]))
                out.append(pltpu.make_async_copy(cv_hbm.at[layer, phys, :, h, :], vbuf.at[h, dst, :], sem.at[1, h]))
        return out

    for h in range(MOBA_HEADS):
        for cp in copies(h):
            cp.start()
    for h in range(MOBA_HEADS):
        for cp in copies(h):
            cp.wait()
        qh = q_ref[0, h:h + 1, :]
        qb = jnp.broadcast_to(qh, (SUBLANES, MOBA_DH)).astype(BF16)
        s = lax.dot_general(qb, kbuf[h].astype(BF16), NT_DIMS, preferred_element_type=F32)
        s_own = jnp.sum(qh * kn_ref[0, h:h + 1, :], axis=-1, keepdims=True)
        m = jnp.maximum(jnp.max(s, axis=-1, keepdims=True), s_own)
        p = jnp.exp(s - m)
        p_own = jnp.exp(s_own - m)
        denom = jnp.sum(p, axis=-1, keepdims=True) + p_own
        o = jnp.dot(p.astype(BF16), vbuf[h].astype(BF16), preferred_element_type=F32) + p_own * vn_ref[0, h:h + 1, :]
        o_ref[0, h:h + 1, :] = (o / denom)[0:1, :]


def _sample_attn(top_flat, page_table_flat, q_heads, k_new, v_new, cache_k, cache_v, *, layer, n_sel):
    n = q_heads.shape[0]
    page_size = cache_k.shape[2]
    n_pages = page_table_flat.shape[0] // n
    rows = n_sel * MOBA_BLOCK
    head_spec = pl.BlockSpec((1, MOBA_HEADS, MOBA_DH), lambda i, tp, pt: (i, 0, 0))
    grid_spec = pltpu.PrefetchScalarGridSpec(
        num_scalar_prefetch=2,
        grid=(n,),
        in_specs=[head_spec, head_spec, head_spec,
                  pl.BlockSpec(memory_space=pl.ANY), pl.BlockSpec(memory_space=pl.ANY)],
        out_specs=head_spec,
        scratch_shapes=[pltpu.VMEM((MOBA_HEADS, rows, MOBA_DH), F32), pltpu.VMEM((MOBA_HEADS, rows, MOBA_DH), F32),
                        pltpu.SemaphoreType.DMA((2, MOBA_HEADS))],
    )
    return pl.pallas_call(
        functools.partial(_sample_attn_kernel, layer=layer, n_pages=n_pages, page_size=page_size, n_sel=n_sel),
        grid_spec=grid_spec,
        out_shape=jax.ShapeDtypeStruct((n, MOBA_HEADS, MOBA_DH), F32),
        compiler_params=pltpu.CompilerParams(dimension_semantics=("arbitrary",)),
        name="sample_attn",
    )(top_flat, page_table_flat, q_heads, k_new, v_new, cache_k, cache_v)


def _merge_kernel(x_ref, go_ref, mo_ref, mg_ref, w1_ref, w2_ref, g_ref, b_ref, y_ref, *, alpha):
    mo = (mo_ref[...] * _silu(mg_ref[...])).astype(BF16)
    z = (alpha * x_ref[...] + jnp.dot(go_ref[...], w1_ref[...], preferred_element_type=F32)
         + jnp.dot(mo, w2_ref[...], preferred_element_type=F32))
    mu = jnp.mean(z, axis=-1, keepdims=True)
    zc = z - mu
    var = jnp.mean(jnp.square(zc), axis=-1, keepdims=True)
    y_ref[...] = zc * lax.rsqrt(var + LN_EPS) * g_ref[...] + b_ref[...]


def _merge(x2d, gla_o, moba_o, mg, w1, w2, ln_g, ln_b, *, tm, alpha):
    n_tok, d_model = x2d.shape
    row = lambda i: (i, 0)
    const = lambda i: (0, 0)
    return pl.pallas_call(
        functools.partial(_merge_kernel, alpha=alpha),
        grid=(n_tok // tm,),
        in_specs=[pl.BlockSpec((tm, d_model), row), pl.BlockSpec((tm, GLA_V), row), pl.BlockSpec((tm, MOBA_W), row),
                  pl.BlockSpec((tm, MOBA_W), row), pl.BlockSpec(w1.shape, const), pl.BlockSpec(w2.shape, const),
                  pl.BlockSpec((1, d_model), const), pl.BlockSpec((1, d_model), const)],
        out_specs=pl.BlockSpec((tm, d_model), row),
        out_shape=jax.ShapeDtypeStruct((n_tok, d_model), F32),
        compiler_params=pltpu.CompilerParams(dimension_semantics=("arbitrary",), vmem_limit_bytes=VMEM_LIMIT_BYTES),
        name="merge",
    )(x2d, gla_o, moba_o, mg, w1, w2, ln_g, ln_b)


def _split_w_in(w_in_l, w_gk2_l, b_gk_l):
    n_gla = 2 * GLA_QK + 2 * GLA_V
    wg = w_in_l[:, :n_gla].astype(BF16)
    wl = jnp.pad(w_in_l[:, n_gla:n_gla + GLA_GATE_RANK], ((0, 0), (0, LANES - GLA_GATE_RANK))).astype(BF16)
    wm = w_in_l[:, n_gla + GLA_GATE_RANK:].astype(BF16)
    wgk2 = jnp.pad(w_gk2_l, ((0, LANES - GLA_GATE_RANK), (0, 0))).astype(BF16)
    return wg, wl, wgk2, b_gk_l.reshape(1, GLA_QK), wm


def kernel(x_prompt, x_sample, cache_k, cache_v, state_gla, page_table, w_in, w_gk2, b_gk, gla_norm_w, w_out,
           ln_g, ln_b):
    batch, seq, d_model = x_prompt.shape
    dec_batch, dec_seq, _ = x_sample.shape
    depth = w_in.shape[0]
    page_size = cache_k.shape[2]
    n_pages = page_table.shape[1]
    past_len = n_pages * page_size
    assert dec_seq == 1 and past_len % MOBA_BLOCK == 0 and MOBA_BLOCK % page_size == 0
    assert seq % MOBA_BLOCK == 0
    nb = seq // MOBA_BLOCK
    assert nb >= MOBA_TOPK and FLAG_LANE0 + nb <= LANES
    nb_past = past_len // MOBA_BLOCK
    n_sel = min(MOBA_TOPK, nb_past)
    alpha = (2.0 * depth) ** 0.25

    half = MOBA_DH // 2
    inv = ROPE_THETA ** (-jnp.arange(half, dtype=F32) / half)
    inv_lane = jnp.tile(inv, LANES // half).reshape(1, LANES)
    pt_flat = page_table.reshape(-1)

    xp = x_prompt.reshape(batch * seq, d_model)
    xs = x_sample.reshape(dec_batch * dec_seq, d_model)
    kp_l, vp_l, sp_l, ks_l, vs_l, ss_l = [], [], [], [], [], []
    for l in range(depth):
        w_parts = _split_w_in(w_in[l], w_gk2[l], b_gk[l])
        norm_w = gla_norm_w[l].reshape(1, GLA_DV)
        w1 = w_out[l, :GLA_V].astype(BF16)
        w2 = w_out[l, GLA_V:].astype(BF16)
        g_row = ln_g[l].reshape(1, d_model)
        b_row = ln_b[l].reshape(1, d_model)

        gq, gk, lf, gv, gg, mq, mk, mv, mg, ksum = _project(
            xp, w_parts, inv_lane, tm=MOBA_BLOCK, seq_len=seq, pos_base=0, block_sums=True)
        ksum = ksum.reshape(batch, nb, SUBLANES, MOBA_W)[:, :, 0, :]
        q_ext, k_ext, v_ext = _moba_prep(mq, mk, mv, ksum, batch=batch, nb=nb)
        moba_p = _moba_attn(q_ext, k_ext, v_ext, batch=batch, nb=nb)
        gla_p, st_p = _gla_prompt(gq, gk, lf, gv, gg, norm_w, batch=batch, seq=seq, tt=MOBA_BLOCK)
        kp_l.append(mk.reshape(batch, seq, MOBA_HEADS, MOBA_DH))
        vp_l.append(mv.reshape(batch, seq, MOBA_HEADS, MOBA_DH))
        sp_l.append(st_p)

        sq, sk, slf, sv, sg, smq, smk, smv, smg = _project(
            xs, w_parts, inv_lane, tm=dec_batch, seq_len=dec_seq, pos_base=past_len, block_sums=False)
        gla_s, st_s = _gla_step(sq, sk, slf, sv, sg, norm_w, state_gla[l])
        q_heads = smq.reshape(dec_batch, MOBA_HEADS, MOBA_DH)
        k_new = smk.reshape(dec_batch, MOBA_HEADS, MOBA_DH)
        v_new = smv.reshape(dec_batch, MOBA_HEADS, MOBA_DH)
        top = _sample_route(pt_flat, q_heads, cache_k, layer=l, nb_past=nb_past)
        top_flat = top[:, :, :n_sel].reshape(-1)
        moba_s = _sample_attn(top_flat, pt_flat, q_heads, k_new, v_new, cache_k, cache_v, layer=l, n_sel=n_sel)
        moba_s = moba_s.reshape(dec_batch, MOBA_W)
        ks_l.append(k_new.reshape(dec_batch, dec_seq, MOBA_HEADS, MOBA_DH))
        vs_l.append(v_new.reshape(dec_batch, dec_seq, MOBA_HEADS, MOBA_DH))
        ss_l.append(st_s)

        xp = _merge(xp, gla_p, moba_p, mg, w1, w2, g_row, b_row, tm=2 * MOBA_BLOCK, alpha=alpha)
        xs = _merge(xs, gla_s, moba_s, smg, w1, w2, g_row, b_row, tm=dec_batch, alpha=alpha)

    return (xp.reshape(batch, seq, d_model), xs.reshape(dec_batch, dec_seq, d_model),
            jnp.stack(kp_l), jnp.stack(vp_l), jnp.stack(sp_l), jnp.stack(ks_l), jnp.stack(vs_l), jnp.stack(ss_l))
```

```python
import functools
import math

import jax
import jax.numpy as jnp
from jax import lax
from jax.experimental import pallas as pl
from jax.experimental.pallas import tpu as pltpu

GLA_HEADS = 4
GLA_DK = 64
GLA_DV = 128
GLA_QK = GLA_HEADS * GLA_DK
GLA_V = GLA_HEADS * GLA_DV
GLA_GATE_RANK = 16
GLA_GATE_NORM = 16.0
GLA_CHUNK = 64
MOBA_HEADS = 8
MOBA_DH = 64
MOBA_W = MOBA_HEADS * MOBA_DH
MOBA_BLOCK = 256
MOBA_TOPK = 3
ROPE_THETA = 10000.0
LN_EPS = 1e-5
RMS_EPS = 1e-6

LANES = 128
SUBLANES = 8
VMEM_LIMIT_BYTES = 48 * 1024 * 1024

MASK_BIG = 2.0 ** 100
FLAG_LANE0 = MOBA_DH
LOG2E = math.log2(math.e)
QK_LOOKAHEAD = 8

F32 = jnp.float32
BF16 = jnp.bfloat16
NT_DIMS = (((1,), (1,)), ((), ()))


def _silu(x):
    return x / (1.0 + jnp.exp(-x))


def _log_sigmoid(z):
    return jnp.minimum(z, 0.0) - jnp.log(1.0 + jnp.exp(-jnp.abs(z)))


def _lane_iota(shape):
    return lax.broadcasted_iota(jnp.int32, shape, len(shape) - 1)


def _expand_heads64(x):
    rows, width = x.shape
    low = _lane_iota((1, LANES)) < MOBA_DH
    parts = []
    for g in range(width // LANES):
        a = x[:, g * LANES:(g + 1) * LANES]
        parts.append(jnp.where(low, a, 0.0))
        parts.append(jnp.where(low, pltpu.roll(a, LANES - MOBA_DH, 1), 0.0))
    return jnp.concatenate(parts, axis=1)


def _compact_heads64(slots):
    low = _lane_iota((1, LANES)) < MOBA_DH
    parts = []
    for g in range(len(slots) // 2):
        parts.append(jnp.where(low, slots[2 * g], pltpu.roll(slots[2 * g + 1], MOBA_DH, 1)))
    return jnp.concatenate(parts, axis=1)


def _proj_kernel(x_ref, wg_ref, wl_ref, wgk2_ref, bgk_ref, wm_ref, inv_ref,
                 gq_ref, gk_ref, lf_ref, gv_ref, gg_ref, mq_ref, mk_ref, mv_ref, mg_ref, *ksum_refs,
                 tm, seq_len, pos_base):
    x = x_ref[...].astype(BF16)

    ug = jnp.dot(x, wg_ref[...], preferred_element_type=F32)
    gq_ref[...] = _expand_heads64(ug[:, :GLA_QK] * (GLA_DK ** -0.5))
    gk_ref[...] = _expand_heads64(ug[:, GLA_QK:2 * GLA_QK])
    gv_ref[...] = ug[:, 2 * GLA_QK:2 * GLA_QK + GLA_V]
    gg_ref[...] = ug[:, 2 * GLA_QK + GLA_V:]

    glow = jnp.dot(x, wl_ref[...], preferred_element_type=F32)
    z = jnp.dot(glow.astype(BF16), wgk2_ref[...], preferred_element_type=F32) + bgk_ref[...]
    lf_ref[...] = _expand_heads64(_log_sigmoid(z) / GLA_GATE_NORM)

    um = jnp.dot(x, wm_ref[...], preferred_element_type=F32)
    if seq_len % tm == 0:
        pos0 = pos_base + lax.rem(pl.program_id(0) * tm, seq_len)
        pos = (pos0 + lax.broadcasted_iota(jnp.int32, (tm, 1), 0)).astype(F32)
    else:
        assert seq_len == 1
        pos = jnp.full((tm, 1), pos_base, F32)
    ang = pos * inv_ref[...]
    cos = jnp.cos(ang)
    sin = jnp.sin(ang)
    first_half = (_lane_iota((1, LANES)) % MOBA_DH) < (MOBA_DH // 2)
    sin_signed = jnp.where(first_half, -sin, sin)

    def rope(u):
        parts = []
        for g in range(MOBA_W // LANES):
            a = u[:, g * LANES:(g + 1) * LANES]
            partner = jnp.where(first_half, pltpu.roll(a, LANES - MOBA_DH // 2, 1), pltpu.roll(a, MOBA_DH // 2, 1))
            parts.append(a * cos + partner * sin_signed)
        return jnp.concatenate(parts, axis=1)

    mq_ref[...] = rope(um[:, :MOBA_W]) * (MOBA_DH ** -0.5)
    mk = rope(um[:, MOBA_W:2 * MOBA_W])
    mk_ref[...] = mk
    mv_ref[...] = um[:, 2 * MOBA_W:3 * MOBA_W]
    mg_ref[...] = um[:, 3 * MOBA_W:]
    if ksum_refs:
        ksum_refs[0][...] = jnp.broadcast_to(jnp.sum(mk, axis=0, keepdims=True), (1, SUBLANES, MOBA_W))


def _project(x2d, w_parts, inv_lane, *, tm, seq_len, pos_base, block_sums):
    n_tok, d_model = x2d.shape
    wg, wl, wgk2, bgk, wm = w_parts
    n_tiles = n_tok // tm
    row = lambda i: (i, 0)
    const = lambda i: (0, 0)
    widths = (2 * GLA_QK, 2 * GLA_QK, 2 * GLA_QK, GLA_V, GLA_V, MOBA_W, MOBA_W, MOBA_W, MOBA_W)
    out_shape = [jax.ShapeDtypeStruct((n_tok, w), F32) for w in widths]
    out_specs = [pl.BlockSpec((tm, w), row) for w in widths]
    if block_sums:
        out_shape.append(jax.ShapeDtypeStruct((n_tiles, SUBLANES, MOBA_W), F32))
        out_specs.append(pl.BlockSpec((1, SUBLANES, MOBA_W), lambda i: (i, 0, 0)))
    return pl.pallas_call(
        functools.partial(_proj_kernel, tm=tm, seq_len=seq_len, pos_base=pos_base),
        grid=(n_tiles,),
        in_specs=[pl.BlockSpec((tm, d_model), row),
                  pl.BlockSpec(wg.shape, const), pl.BlockSpec(wl.shape, const), pl.BlockSpec(wgk2.shape, const),
                  pl.BlockSpec(bgk.shape, const), pl.BlockSpec(wm.shape, const), pl.BlockSpec(inv_lane.shape, const)],
        out_specs=out_specs,
        out_shape=out_shape,
        compiler_params=pltpu.CompilerParams(dimension_semantics=("arbitrary",), vmem_limit_bytes=VMEM_LIMIT_BYTES),
        name="proj",
    )(x2d, wg, wl, wgk2, bgk, wm, inv_lane)


def _topk_rows(gate, jrow, n_valid_limit):
    sel = jnp.zeros(gate.shape, F32)
    cur = gate
    for r in range(MOBA_TOPK):
        m = jnp.max(cur, axis=0, keepdims=True)
        idx = jnp.min(jnp.where(cur == m, jrow, 1e9), axis=0, keepdims=True)
        pick = jrow == idx
        counted = jnp.where(r < n_valid_limit, 1.0, 0.0)
        sel = jnp.where(pick, jnp.maximum(sel, counted), sel)
        cur = jnp.where(pick, -jnp.inf, cur)
    return sel


def _moba_prep_kernel(q_ref, k_ref, v_ref, ksum_ref, qe_ref, ke_ref, vet_ref, *, nb):
    i = pl.program_id(1)
    kmean = ksum_ref[0] * (1.0 / MOBA_BLOCK)
    lane = _lane_iota((1, LANES))
    jrow = lax.broadcasted_iota(jnp.int32, (nb, MOBA_BLOCK), 0).astype(F32)
    past = jrow < i.astype(F32)
    q_slots = []
    k_slots = []
    v_slots = []
    km_slots = []
    for ref_or_val, slots in ((q_ref[...], q_slots), (k_ref[...], k_slots), (v_ref[...], v_slots), (kmean, km_slots)):
        e = _expand_heads64(ref_or_val)
        for h in range(MOBA_HEADS):
            slots.append(e[:, h * LANES:(h + 1) * LANES])
    for h in range(MOBA_HEADS):
        gate = lax.dot_general(km_slots[h], q_slots[h], NT_DIMS, precision=lax.Precision.HIGHEST,
                               preferred_element_type=F32)
        sel = _topk_rows(jnp.where(past, gate, -jnp.inf), jrow, i)
        notsel = jnp.where(past, 1.0 - sel, 0.0)
        pre = jnp.concatenate([jnp.zeros((FLAG_LANE0, MOBA_BLOCK), F32), notsel,
                               jnp.zeros((LANES - FLAG_LANE0 - nb, MOBA_BLOCK), F32)], axis=0)
        flags = pre.T
        sl = slice(h * LANES, (h + 1) * LANES)
        qe_ref[:, sl] = (q_slots[h] * LOG2E + flags).astype(BF16)
        ke_ref[:, sl] = jnp.where(lane == FLAG_LANE0 + i, -MASK_BIG, k_slots[h]).astype(BF16)
        vet_ref[0, sl, :] = jnp.where(lane == FLAG_LANE0, 1.0, v_slots[h]).T.astype(BF16)


def _moba_prep(mq, mk, mv, ksum, *, batch, nb):
    n_tok = mq.shape[0]
    row = lambda b, i: (b * nb + i, 0)
    width = MOBA_HEADS * LANES
    ext = jax.ShapeDtypeStruct((n_tok, width), BF16)
    return pl.pallas_call(
        functools.partial(_moba_prep_kernel, nb=nb),
        grid=(batch, nb),
        in_specs=[pl.BlockSpec((MOBA_BLOCK, MOBA_W), row)] * 3 + [pl.BlockSpec((1, nb, MOBA_W), lambda b, i: (b, 0, 0))],
        out_specs=[pl.BlockSpec((MOBA_BLOCK, width), row)] * 2
                  + [pl.BlockSpec((1, width, MOBA_BLOCK), lambda b, i: (b, 0, i))],
        out_shape=[ext, ext, jax.ShapeDtypeStruct((batch, width, nb * MOBA_BLOCK), BF16)],
        compiler_params=pltpu.CompilerParams(dimension_semantics=("arbitrary", "arbitrary"),
                                             vmem_limit_bytes=VMEM_LIMIT_BYTES),
        name="moba_prep",
    )(mq, mk, mv, ksum)


def _moba_attn_kernel(itab_ref, jtab_ref, q_ref, k_ref, vt_ref, o_ref, m_sc, acc_sc):
    p = pl.program_id(1)
    i = itab_ref[p]
    j = jtab_ref[p]

    def scores(h):
        sl = slice(h * LANES, (h + 1) * LANES)
        return lax.dot_general(k_ref[:, sl], q_ref[:, sl], NT_DIMS, preferred_element_type=F32)

    def step(causal):
        if causal:
            keep = (lax.broadcasted_iota(jnp.int32, (MOBA_BLOCK, MOBA_BLOCK), 0)
                    <= lax.broadcasted_iota(jnp.int32, (MOBA_BLOCK, MOBA_BLOCK), 1))
        pending = [scores(h) for h in range(QK_LOOKAHEAD)]
        for h in range(MOBA_HEADS):
            s = pending.pop(0)
            if h + QK_LOOKAHEAD < MOBA_HEADS:
                pending.append(scores(h + QK_LOOKAHEAD))
            sl = slice(h * LANES, (h + 1) * LANES)
            if causal:
                s = jnp.where(keep, s, -jnp.inf)
                m_new = jnp.max(s, axis=0, keepdims=True)
            else:
                m_old = m_sc[h:h + 1, :]
                m_new = jnp.maximum(m_old, jnp.max(s, axis=0, keepdims=True))
            pexp = jnp.exp2(s - m_new)
            pv = jnp.dot(vt_ref[0, sl, :], pexp.astype(BF16), preferred_element_type=F32)
            if causal:
                acc_sc[sl, :] = pv
            else:
                acc_sc[sl, :] = jnp.exp2(m_old - m_new) * acc_sc[sl, :] + pv
            m_sc[h:h + 1, :] = m_new

    @pl.when(j == i)
    def _():
        step(True)

    @pl.when(j != i)
    def _():
        step(False)

    @pl.when((j == i - 1) | (i == 0))
    def _():
        slots = []
        for h in range(MOBA_HEADS):
            acc = acc_sc[h * LANES:(h + 1) * LANES, :]
            slots.append((acc / acc[FLAG_LANE0:FLAG_LANE0 + 1, :]).T)
        o_ref[...] = _compact_heads64(slots)


def _moba_attn(q_ext, k_ext, v_ext_t, *, batch, nb):
    n_tok = q_ext.shape[0]
    pairs_i = []
    pairs_j = []
    for i in range(nb):
        for j in [i] + list(range(i)):
            pairs_i.append(i)
            pairs_j.append(j)
    itab = jnp.asarray(pairs_i, jnp.int32)
    jtab = jnp.asarray(pairs_j, jnp.int32)
    width = MOBA_HEADS * LANES
    grid_spec = pltpu.PrefetchScalarGridSpec(
        num_scalar_prefetch=2,
        grid=(batch, len(pairs_i)),
        in_specs=[pl.BlockSpec((MOBA_BLOCK, width), lambda b, p, it, jt: (b * nb + it[p], 0)),
                  pl.BlockSpec((MOBA_BLOCK, width), lambda b, p, it, jt: (b * nb + jt[p], 0)),
                  pl.BlockSpec((1, width, MOBA_BLOCK), lambda b, p, it, jt: (b, 0, jt[p]))],
        out_specs=pl.BlockSpec((MOBA_BLOCK, MOBA_W), lambda b, p, it, jt: (b * nb + it[p], 0)),
        scratch_shapes=[pltpu.VMEM((MOBA_HEADS, MOBA_BLOCK), F32), pltpu.VMEM((width, MOBA_BLOCK), F32)],
    )
    return pl.pallas_call(
        _moba_attn_kernel,
        grid_spec=grid_spec,
        out_shape=jax.ShapeDtypeStruct((n_tok, MOBA_W), F32),
        compiler_params=pltpu.CompilerParams(dimension_semantics=("arbitrary", "arbitrary"),
                                             vmem_limit_bytes=VMEM_LIMIT_BYTES),
        name="moba_attn",
    )(itab, jtab, q_ext, k_ext, v_ext_t)


def _split3_bf16(x):
    x1 = x.astype(BF16)
    r1 = x - x1.astype(F32)
    x2 = r1.astype(BF16)
    x3 = (r1 - x2.astype(F32)).astype(BF16)
    return x1, x2, x3


def _gla_norm_gate(o, nw, gg):
    on = o * lax.rsqrt(jnp.mean(jnp.square(o), axis=-1, keepdims=True) + RMS_EPS) * nw
    return on * _silu(gg)


def _gla_kernel(q_ref, k_ref, lf_ref, v_ref, g_ref, nw_ref, o_ref, st_ref, st_sc, *, tt):
    t = pl.program_id(1)

    @pl.when(t == 0)
    def _():
        st_sc[...] = jnp.zeros(st_sc.shape, F32)

    c = GLA_CHUNK
    tril = lax.broadcasted_iota(jnp.int32, (c, c), 1) <= lax.broadcasted_iota(jnp.int32, (c, c), 0)
    ltri = jnp.where(tril, 1.0, 0.0).astype(BF16)
    nw = nw_ref[...]
    for ci in range(tt // c):
        rows = slice(ci * c, (ci + 1) * c)
        for h in range(GLA_HEADS):
            sl = slice(h * LANES, (h + 1) * LANES)
            g1, g2, g3 = _split3_bf16(lf_ref[rows, sl])
            b = (jnp.dot(ltri, g1, preferred_element_type=F32) + jnp.dot(ltri, g2, preferred_element_type=F32)
                 + jnp.dot(ltri, g3, preferred_element_type=F32))
            b_last = b[c - 1:c, :]
            q = q_ref[rows, sl]
            k = k_ref[rows, sl]
            v = v_ref[rows, sl]
            qd = (q * jnp.exp(b)).astype(BF16)
            kd = (k * jnp.exp(-b)).astype(BF16)
            kdec = (k * jnp.exp(b_last - b)).astype(BF16)
            a = lax.dot_general(qd, kd, NT_DIMS, preferred_element_type=F32)
            a = jnp.where(tril, a, 0.0)
            st = st_sc[h]
            o = (lax.dot_general(qd, st.astype(BF16), NT_DIMS, preferred_element_type=F32)
                 + jnp.dot(a.astype(BF16), v.astype(BF16), preferred_element_type=F32))
            st_sc[h] = st * jnp.exp(b_last) + jnp.dot(v.T.astype(BF16), kdec, preferred_element_type=F32)
            o_ref[rows, sl] = _gla_norm_gate(o, nw, g_ref[rows, sl]).astype(o_ref.dtype)

    @pl.when(t == pl.num_programs(1) - 1)
    def _():
        for h in range(GLA_HEADS):
            st_ref[0, h] = st_sc[h].T[:GLA_DK, :]


def _gla_prompt(gq, gk, lf, gv, gg, norm_w, *, batch, seq, tt):
    n_tok = gq.shape[0]
    nt = seq // tt
    row = lambda b, t: (b * nt + t, 0)
    pad_w = GLA_HEADS * LANES
    return pl.pallas_call(
        functools.partial(_gla_kernel, tt=tt),
        grid=(batch, nt),
        in_specs=[pl.BlockSpec((tt, pad_w), row)] * 3 + [pl.BlockSpec((tt, GLA_V), row)] * 2
                 + [pl.BlockSpec((1, GLA_DV), lambda b, t: (0, 0))],
        out_specs=[pl.BlockSpec((tt, GLA_V), row),
                   pl.BlockSpec((1, GLA_HEADS, GLA_DK, GLA_DV), lambda b, t: (b, 0, 0, 0))],
        out_shape=[jax.ShapeDtypeStruct((n_tok, GLA_V), BF16),
                   jax.ShapeDtypeStruct((batch, GLA_HEADS, GLA_DK, GLA_DV), F32)],
        scratch_shapes=[pltpu.VMEM((GLA_HEADS, GLA_DV, LANES), F32)],
        compiler_params=pltpu.CompilerParams(dimension_semantics=("arbitrary", "arbitrary"),
                                             vmem_limit_bytes=VMEM_LIMIT_BYTES),
        name="gla_prompt",
    )(gq, gk, lf, gv, gg, norm_w)


def _gla_step_kernel(q_ref, k_ref, lf_ref, v_ref, g_ref, nw_ref, s_ref, o_ref, so_ref):
    eye = (lax.broadcasted_iota(jnp.int32, (GLA_DK, LANES), 0) == lax.broadcasted_iota(jnp.int32, (GLA_DK, LANES), 1))

    def column(rowvec):
        return jnp.sum(jnp.where(eye, rowvec, 0.0), axis=1, keepdims=True)

    nw = nw_ref[...]
    for h in range(GLA_HEADS):
        sl = slice(h * LANES, (h + 1) * LANES)
        decay = column(jnp.exp(lf_ref[0, :, sl]))
        kcol = column(k_ref[0, :, sl])
        qcol = column(q_ref[0, :, sl])
        s_new = decay * s_ref[0, h] + kcol * v_ref[0, :, sl]
        so_ref[0, h] = s_new
        o = jnp.sum(qcol * s_new, axis=0, keepdims=True)
        o_ref[0, :, sl] = _gla_norm_gate(o, nw, g_ref[0, :, sl]).astype(o_ref.dtype)


def _gla_step(gq, gk, lf, gv, gg, norm_w, state):
    n = gq.shape[0]
    pad_w = GLA_HEADS * LANES
    as3 = lambda a: a.reshape(n, 1, a.shape[-1])
    vec = lambda w: pl.BlockSpec((1, 1, w), lambda i: (i, 0, 0))
    st_spec = pl.BlockSpec((1, GLA_HEADS, GLA_DK, GLA_DV), lambda i: (i, 0, 0, 0))
    o, s_out = pl.pallas_call(
        _gla_step_kernel,
        grid=(n,),
        in_specs=[vec(pad_w)] * 3 + [vec(GLA_V)] * 2 + [pl.BlockSpec((1, GLA_DV), lambda i: (0, 0)), st_spec],
        out_specs=[vec(GLA_V), st_spec],
        out_shape=[jax.ShapeDtypeStruct((n, 1, GLA_V), BF16), jax.ShapeDtypeStruct(state.shape, F32)],
        compiler_params=pltpu.CompilerParams(dimension_semantics=("arbitrary",)),
        name="gla_step",
    )(as3(gq), as3(gk), as3(lf), as3(gv), as3(gg), norm_w, state)
    return o.reshape(n, GLA_V), s_out


ROUTE_PAGES_IN_FLIGHT = 16


def _sample_route_kernel(pt_ref, q_ref, ckt_hbm, top_ref, buf, sem, qb_sc, gp_sc,
                         *, layer, n_pages, page_size, nb_past):
    n = pl.program_id(0)
    n_seq = pl.num_programs(0)
    pages_per_block = MOBA_BLOCK // page_size
    blocks_in_flight = ROUTE_PAGES_IN_FLIGHT // pages_per_block

    def page_copy(seq, blk, a):
        slot = lax.rem(blk, blocks_in_flight) * pages_per_block + a
        phys = pt_ref[seq * n_pages + blk * pages_per_block + a]
        return pltpu.make_async_copy(ckt_hbm.at[layer, phys], buf.at[slot], sem.at[slot])

    @pl.when(n == 0)
    def _():
        for blk in range(blocks_in_flight):
            for a in range(pages_per_block):
                page_copy(n, jnp.int32(blk), a).start()

    eye = (lax.broadcasted_iota(jnp.int32, (MOBA_DH, MOBA_DH), 0)
           == lax.broadcasted_iota(jnp.int32, (MOBA_DH, MOBA_DH), 1))
    for h in range(MOBA_HEADS):
        qcol = jnp.sum(jnp.where(eye, q_ref[0, h:h + 1, :], 0.0), axis=1, keepdims=True)
        qb_sc[h] = jnp.broadcast_to(qcol, (MOBA_DH, page_size))

    def body(blk, carry):
        total = None
        for a in range(pages_per_block):
            page_copy(n, blk, a).wait()
            slot = lax.rem(blk, blocks_in_flight) * pages_per_block + a
            total = buf[slot] if total is None else total + buf[slot]
        gp_sc[blk] = jnp.sum(total * qb_sc[...], axis=1)
        nxt = blk + blocks_in_flight

        @pl.when(nxt < nb_past)
        def _():
            for a in range(pages_per_block):
                page_copy(n, nxt, a).start()

        @pl.when((nxt >= nb_past) & (n + 1 < n_seq))
        def _():
            for a in range(pages_per_block):
                page_copy(n + 1, nxt - nb_past, a).start()
        return carry

    lax.fori_loop(0, nb_past, body, 0)

    gate = jnp.sum(gp_sc[...], axis=-1, keepdims=True) * (1.0 / MOBA_BLOCK)
    jidx = lax.broadcasted_iota(jnp.int32, gate.shape, 0).astype(F32)
    lane = _lane_iota((MOBA_HEADS, LANES))
    out = jnp.zeros((MOBA_HEADS, LANES), jnp.int32)
    cur = gate
    for r in range(min(MOBA_TOPK, nb_past)):
        m = jnp.max(cur, axis=0, keepdims=True)
        idx = jnp.min(jnp.where(cur == m, jidx, 1e9), axis=0, keepdims=True)
        cur = jnp.where(jidx == idx, -jnp.inf, cur)
        out = jnp.where(lane == r, idx[0].astype(jnp.int32), out)
    top_ref[0] = out


def _sample_route(page_table_flat, q_heads, cache_kt, *, layer, nb_past):
    n = q_heads.shape[0]
    page_size = cache_kt.shape[4]
    n_pages = page_table_flat.shape[0] // n
    pages_per_block = MOBA_BLOCK // page_size
    assert nb_past % (ROUTE_PAGES_IN_FLIGHT // pages_per_block) == 0
    grid_spec = pltpu.PrefetchScalarGridSpec(
        num_scalar_prefetch=1,
        grid=(n,),
        in_specs=[pl.BlockSpec((1, MOBA_HEADS, MOBA_DH), lambda i, pt: (i, 0, 0)),
                  pl.BlockSpec(memory_space=pl.ANY)],
        out_specs=pl.BlockSpec((1, MOBA_HEADS, LANES), lambda i, pt: (i, 0, 0)),
        scratch_shapes=[pltpu.VMEM((ROUTE_PAGES_IN_FLIGHT, MOBA_HEADS, MOBA_DH, page_size), F32),
                        pltpu.SemaphoreType.DMA((ROUTE_PAGES_IN_FLIGHT,)),
                        pltpu.VMEM((MOBA_HEADS, MOBA_DH, page_size), F32),
                        pltpu.VMEM((nb_past, MOBA_HEADS, page_size), F32)],
    )
    return pl.pallas_call(
        functools.partial(_sample_route_kernel, layer=layer, n_pages=n_pages, page_size=page_size, nb_past=nb_past),
        grid_spec=grid_spec,
        out_shape=jax.ShapeDtypeStruct((n, MOBA_HEADS, LANES), jnp.int32),
        compiler_params=pltpu.CompilerParams(dimension_semantics=("arbitrary",)),
        name="sample_route",
    )(page_table_flat, q_heads, cache_kt)


def _sample_attn_kernel(top_ref, pt_ref, q_ref, kn_ref, vn_ref, ckt_hbm, cvt_hbm, o_ref, kbuf, vbuf, sem,
                        *, layer, n_pages, page_size, n_sel):
    n = pl.program_id(0)
    n_seq = pl.num_programs(0)
    pages_per_block = MOBA_BLOCK // page_size

    def copies(seq, h):
        half = lax.rem(seq, 2)
        out = []
        for r in range(n_sel):
            blk = top_ref[(seq * MOBA_HEADS + h) * n_sel + r]
            for a in range(pages_per_block):
                phys = pt_ref[seq * n_pages + blk * pages_per_block + a]
                dst = pl.ds((r * pages_per_block + a) * page_size, page_size)
                out.append(pltpu.make_async_copy(ckt_hbm.at[layer, phys, h], kbuf.at[half, h, :, dst],
                                                 sem.at[half, 0, h]))
                out.append(pltpu.make_async_copy(cvt_hbm.at[layer, phys, h], vbuf.at[half, h, :, dst],
                                                 sem.at[half, 1, h]))
        return out

    @pl.when(n == 0)
    def _():
        for h in range(MOBA_HEADS):
            for cp in copies(n, h):
                cp.start()

    @pl.when(n + 1 < n_seq)
    def _():
        for h in range(MOBA_HEADS):
            for cp in copies(n + 1, h):
                cp.start()

    half = lax.rem(n, 2)
    for h in range(MOBA_HEADS):
        for cp in copies(n, h):
            cp.wait()
        qh = q_ref[0, h:h + 1, :]
        qb = jnp.broadcast_to(qh, (SUBLANES, MOBA_DH)).astype(BF16)
        s = jnp.dot(qb, kbuf[half, h].astype(BF16), preferred_element_type=F32)
        s_own = jnp.sum(qh * kn_ref[0, h:h + 1, :], axis=-1, keepdims=True)
        m = jnp.maximum(jnp.max(s, axis=-1, keepdims=True), s_own)
        p = jnp.exp(s - m)
        p_own = jnp.exp(s_own - m)
        denom = jnp.sum(p, axis=-1, keepdims=True) + p_own
        o = (lax.dot_general(p.astype(BF16), vbuf[half, h].astype(BF16), NT_DIMS, preferred_element_type=F32)
             + p_own * vn_ref[0, h:h + 1, :])
        o_ref[0, h:h + 1, :] = (o / denom)[0:1, :]


def _sample_attn(top_flat, page_table_flat, q_heads, k_new, v_new, cache_kt, cache_vt, *, layer, n_sel):
    n = q_heads.shape[0]
    page_size = cache_kt.shape[4]
    n_pages = page_table_flat.shape[0] // n
    rows = n_sel * MOBA_BLOCK
    head_spec = pl.BlockSpec((1, MOBA_HEADS, MOBA_DH), lambda i, tp, pt: (i, 0, 0))
    grid_spec = pltpu.PrefetchScalarGridSpec(
        num_scalar_prefetch=2,
        grid=(n,),
        in_specs=[head_spec, head_spec, head_spec,
                  pl.BlockSpec(memory_space=pl.ANY), pl.BlockSpec(memory_space=pl.ANY)],
        out_specs=head_spec,
        scratch_shapes=[pltpu.VMEM((2, MOBA_HEADS, MOBA_DH, rows), F32), pltpu.VMEM((2, MOBA_HEADS, MOBA_DH, rows), F32),
                        pltpu.SemaphoreType.DMA((2, 2, MOBA_HEADS))],
    )
    return pl.pallas_call(
        functools.partial(_sample_attn_kernel, layer=layer, n_pages=n_pages, page_size=page_size, n_sel=n_sel),
        grid_spec=grid_spec,
        out_shape=jax.ShapeDtypeStruct((n, MOBA_HEADS, MOBA_DH), F32),
        compiler_params=pltpu.CompilerParams(dimension_semantics=("arbitrary",)),
        name="sample_attn",
    )(top_flat, page_table_flat, q_heads, k_new, v_new, cache_kt, cache_vt)


def _merge_kernel(x_ref, go_ref, mo_ref, mg_ref, w1_ref, w2_ref, g_ref, b_ref, y_ref, *, alpha):
    mo = (mo_ref[...] * _silu(mg_ref[...])).astype(BF16)
    z = (alpha * x_ref[...] + jnp.dot(go_ref[...], w1_ref[...], preferred_element_type=F32)
         + jnp.dot(mo, w2_ref[...], preferred_element_type=F32))
    mu = jnp.mean(z, axis=-1, keepdims=True)
    zc = z - mu
    var = jnp.mean(jnp.square(zc), axis=-1, keepdims=True)
    y_ref[...] = zc * lax.rsqrt(var + LN_EPS) * g_ref[...] + b_ref[...]


def _merge(x2d, gla_o, moba_o, mg, w1, w2, ln_g, ln_b, *, tm, alpha):
    n_tok, d_model = x2d.shape
    row = lambda i: (i, 0)
    const = lambda i: (0, 0)
    return pl.pallas_call(
        functools.partial(_merge_kernel, alpha=alpha),
        grid=(n_tok // tm,),
        in_specs=[pl.BlockSpec((tm, d_model), row), pl.BlockSpec((tm, GLA_V), row), pl.BlockSpec((tm, MOBA_W), row),
                  pl.BlockSpec((tm, MOBA_W), row), pl.BlockSpec(w1.shape, const), pl.BlockSpec(w2.shape, const),
                  pl.BlockSpec((1, d_model), const), pl.BlockSpec((1, d_model), const)],
        out_specs=pl.BlockSpec((tm, d_model), row),
        out_shape=jax.ShapeDtypeStruct((n_tok, d_model), F32),
        compiler_params=pltpu.CompilerParams(dimension_semantics=("arbitrary",), vmem_limit_bytes=VMEM_LIMIT_BYTES),
        name="merge",
    )(x2d, gla_o, moba_o, mg, w1, w2, ln_g, ln_b)


def _split_w_in(w_in_l, w_gk2_l, b_gk_l):
    n_gla = 2 * GLA_QK + 2 * GLA_V
    wg = w_in_l[:, :n_gla].astype(BF16)
    wl = jnp.pad(w_in_l[:, n_gla:n_gla + GLA_GATE_RANK], ((0, 0), (0, LANES - GLA_GATE_RANK))).astype(BF16)
    wm = w_in_l[:, n_gla + GLA_GATE_RANK:].astype(BF16)
    wgk2 = jnp.pad(w_gk2_l, ((0, LANES - GLA_GATE_RANK), (0, 0))).astype(BF16)
    return wg, wl, wgk2, b_gk_l.reshape(1, GLA_QK), wm


def kernel(x_prompt, x_sample, cache_k, cache_v, state_gla, page_table, w_in, w_gk2, b_gk, gla_norm_w, w_out,
           ln_g, ln_b):
    batch, seq, d_model = x_prompt.shape
    dec_batch, dec_seq, _ = x_sample.shape
    depth = w_in.shape[0]
    page_size = cache_k.shape[2]
    n_pages = page_table.shape[1]
    past_len = n_pages * page_size
    assert dec_seq == 1 and past_len % MOBA_BLOCK == 0 and MOBA_BLOCK % page_size == 0
    assert seq % MOBA_BLOCK == 0
    nb = seq // MOBA_BLOCK
    assert nb >= MOBA_TOPK and FLAG_LANE0 + nb <= LANES
    nb_past = past_len // MOBA_BLOCK
    n_sel = min(MOBA_TOPK, nb_past)
    alpha = (2.0 * depth) ** 0.25

    half = MOBA_DH // 2
    inv = ROPE_THETA ** (-jnp.arange(half, dtype=F32) / half)
    inv_lane = jnp.tile(inv, LANES // half).reshape(1, LANES)
    pt_flat = page_table.reshape(-1)
    cache_kt = jnp.transpose(cache_k, (0, 1, 3, 4, 2))
    cache_vt = jnp.transpose(cache_v, (0, 1, 3, 4, 2))

    xp = x_prompt.reshape(batch * seq, d_model)
    xs = x_sample.reshape(dec_batch * dec_seq, d_model)
    kp_l, vp_l, sp_l, ks_l, vs_l, ss_l = [], [], [], [], [], []
    for l in range(depth):
        w_parts = _split_w_in(w_in[l], w_gk2[l], b_gk[l])
        norm_w = gla_norm_w[l].reshape(1, GLA_DV)
        w1 = w_out[l, :GLA_V].astype(BF16)
        w2 = w_out[l, GLA_V:].astype(BF16)
        g_row = ln_g[l].reshape(1, d_model)
        b_row = ln_b[l].reshape(1, d_model)

        gq, gk, lf, gv, gg, mq, mk, mv, mg, ksum = _project(
            xp, w_parts, inv_lane, tm=MOBA_BLOCK, seq_len=seq, pos_base=0, block_sums=True)
        ksum = ksum.reshape(batch, nb, SUBLANES, MOBA_W)[:, :, 0, :]
        q_ext, k_ext, v_ext_t = _moba_prep(mq, mk, mv, ksum, batch=batch, nb=nb)
        moba_p = _moba_attn(q_ext, k_ext, v_ext_t, batch=batch, nb=nb)
        gla_p, st_p = _gla_prompt(gq, gk, lf, gv, gg, norm_w, batch=batch, seq=seq, tt=MOBA_BLOCK)
        kp_l.append(mk.reshape(batch, seq, MOBA_HEADS, MOBA_DH))
        vp_l.append(mv.reshape(batch, seq, MOBA_HEADS, MOBA_DH))
        sp_l.append(st_p)

        sq, sk, slf, sv, sg, smq, smk, smv, smg = _project(
            xs, w_parts, inv_lane, tm=dec_batch, seq_len=dec_seq, pos_base=past_len, block_sums=False)
        gla_s, st_s = _gla_step(sq, sk, slf, sv, sg, norm_w, state_gla[l])
        q_heads = smq.reshape(dec_batch, MOBA_HEADS, MOBA_DH)
        k_new = smk.reshape(dec_batch, MOBA_HEADS, MOBA_DH)
        v_new = smv.reshape(dec_batch, MOBA_HEADS, MOBA_DH)
        top = _sample_route(pt_flat, q_heads, cache_kt, layer=l, nb_past=nb_past)
        top_flat = top[:, :, :n_sel].reshape(-1)
        moba_s = _sample_attn(top_flat, pt_flat, q_heads, k_new, v_new, cache_kt, cache_vt, layer=l, n_sel=n_sel)
        moba_s = moba_s.reshape(dec_batch, MOBA_W)
        ks_l.append(k_new.reshape(dec_batch, dec_seq, MOBA_HEADS, MOBA_DH))
        vs_l.append(v_new.reshape(dec_batch, dec_seq, MOBA_HEADS, MOBA_DH))
        ss_l.append(st_s)

        xp = _merge(xp, gla_p, moba_p, mg, w1, w2, g_row, b_row, tm=2 * MOBA_BLOCK, alpha=alpha)
        xs = _merge(xs, gla_s, moba_s, smg, w1, w2, g_row, b_row, tm=dec_batch, alpha=alpha)

    return (xp.reshape(batch, seq, d_model), xs.reshape(dec_batch, dec_seq, d_model),
            jnp.stack(kp_l), jnp.stack(vp_l), jnp.stack(sp_l), jnp.stack(ks_l), jnp.stack(vs_l), jnp.stack(ss_l))
```

```python
import functools
import math

import jax
import jax.numpy as jnp
from jax import lax
from jax.experimental import pallas as pl
from jax.experimental.pallas import tpu as pltpu

GLA_HEADS = 4
GLA_DK = 64
GLA_DV = 128
GLA_QK = GLA_HEADS * GLA_DK
GLA_V = GLA_HEADS * GLA_DV
GLA_GATE_RANK = 16
GLA_GATE_NORM = 16.0
GLA_CHUNK = 64
MOBA_HEADS = 8
MOBA_DH = 64
MOBA_W = MOBA_HEADS * MOBA_DH
MOBA_BLOCK = 256
MOBA_TOPK = 3
ROPE_THETA = 10000.0
LN_EPS = 1e-5
RMS_EPS = 1e-6

LANES = 128
SUBLANES = 8
VMEM_LIMIT_BYTES = 48 * 1024 * 1024

MASK_BIG = 2.0 ** 100
FLAG_LANE0 = MOBA_DH
LOG2E = math.log2(math.e)
QK_LOOKAHEAD = 8
GLA_TOKENS_PER_STEP = 512
ATTN_Q_BLOCKS = 2

F32 = jnp.float32
BF16 = jnp.bfloat16
NT_DIMS = (((1,), (1,)), ((), ()))


def _silu(x):
    return x / (1.0 + jnp.exp(-x))


def _log_sigmoid(z):
    return jnp.minimum(z, 0.0) - jnp.log(1.0 + jnp.exp(-jnp.abs(z)))


def _lane_iota(shape):
    return lax.broadcasted_iota(jnp.int32, shape, len(shape) - 1)


def _expand_heads64(x):
    rows, width = x.shape
    low = _lane_iota((1, LANES)) < MOBA_DH
    parts = []
    for g in range(width // LANES):
        a = x[:, g * LANES:(g + 1) * LANES]
        parts.append(jnp.where(low, a, 0.0))
        parts.append(jnp.where(low, pltpu.roll(a, LANES - MOBA_DH, 1), 0.0))
    return jnp.concatenate(parts, axis=1)


def _compact_heads64(slots):
    low = _lane_iota((1, LANES)) < MOBA_DH
    parts = []
    for g in range(len(slots) // 2):
        parts.append(jnp.where(low, slots[2 * g], pltpu.roll(slots[2 * g + 1], MOBA_DH, 1)))
    return jnp.concatenate(parts, axis=1)


def _proj_kernel(x_ref, wg_ref, wl_ref, wgk2_ref, bgk_ref, wm_ref, inv_ref,
                 gq_ref, gk_ref, lf_ref, gv_ref, gg_ref, mq_ref, mk_ref, mg_ref, kt_ref, vt_ref, *opt_refs,
                 tm, seq_len, pos_base, block_sums):
    x = x_ref[...].astype(BF16)

    ug = jnp.dot(x, wg_ref[...], preferred_element_type=F32)
    gq_ref[...] = _expand_heads64(ug[:, :GLA_QK] * (GLA_DK ** -0.5))
    gk_ref[...] = _expand_heads64(ug[:, GLA_QK:2 * GLA_QK])
    gv_ref[...] = ug[:, 2 * GLA_QK:2 * GLA_QK + GLA_V]
    gg_ref[...] = ug[:, 2 * GLA_QK + GLA_V:]

    glow = jnp.dot(x, wl_ref[...], preferred_element_type=F32)
    z = jnp.dot(glow.astype(BF16), wgk2_ref[...], preferred_element_type=F32) + bgk_ref[...]
    lf_ref[...] = _expand_heads64(_log_sigmoid(z) / GLA_GATE_NORM)

    um = jnp.dot(x, wm_ref[...], preferred_element_type=F32)
    if seq_len % tm == 0:
        pos0 = pos_base + lax.rem(pl.program_id(0) * tm, seq_len)
        pos = (pos0 + lax.broadcasted_iota(jnp.int32, (tm, 1), 0)).astype(F32)
    else:
        assert seq_len == 1
        pos = jnp.full((tm, 1), pos_base, F32)
    ang = pos * inv_ref[...]
    cos = jnp.cos(ang)
    sin = jnp.sin(ang)
    first_half = (_lane_iota((1, LANES)) % MOBA_DH) < (MOBA_DH // 2)
    sin_signed = jnp.where(first_half, -sin, sin)

    def rope(u):
        parts = []
        for g in range(MOBA_W // LANES):
            a = u[:, g * LANES:(g + 1) * LANES]
            partner = jnp.where(first_half, pltpu.roll(a, LANES - MOBA_DH // 2, 1), pltpu.roll(a, MOBA_DH // 2, 1))
            parts.append(a * cos + partner * sin_signed)
        return jnp.concatenate(parts, axis=1)

    mq_ref[...] = rope(um[:, :MOBA_W]) * (MOBA_DH ** -0.5)
    mk = rope(um[:, MOBA_W:2 * MOBA_W])
    mv = um[:, 2 * MOBA_W:3 * MOBA_W]
    mk_ref[...] = mk
    mg_ref[...] = um[:, 3 * MOBA_W:]
    kt_ref[0] = mk.T
    vt_ref[0] = mv.T
    if block_sums:
        opt_refs[0][...] = jnp.broadcast_to(jnp.sum(mk, axis=0, keepdims=True), (1, SUBLANES, MOBA_W))
    else:
        opt_refs[0][...] = mv


def _project(x2d, w_parts, inv_lane, *, tm, seq_len, pos_base, block_sums):
    n_tok, d_model = x2d.shape
    wg, wl, wgk2, bgk, wm = w_parts
    n_tiles = n_tok // tm
    tiles_per_seq = max(seq_len // tm, 1)
    row = lambda i: (i, 0)
    const = lambda i: (0, 0)
    widths = (2 * GLA_QK, 2 * GLA_QK, 2 * GLA_QK, GLA_V, GLA_V, MOBA_W, MOBA_W, MOBA_W)
    out_shape = [jax.ShapeDtypeStruct((n_tok, w), F32) for w in widths]
    out_specs = [pl.BlockSpec((tm, w), row) for w in widths]
    for _ in range(2):
        out_shape.append(jax.ShapeDtypeStruct((n_tiles // tiles_per_seq, MOBA_W, tiles_per_seq * tm), F32))
        out_specs.append(pl.BlockSpec((1, MOBA_W, tm), lambda i: (i // tiles_per_seq, 0, i % tiles_per_seq)))
    if block_sums:
        out_shape.append(jax.ShapeDtypeStruct((n_tiles, SUBLANES, MOBA_W), F32))
        out_specs.append(pl.BlockSpec((1, SUBLANES, MOBA_W), lambda i: (i, 0, 0)))
    else:
        out_shape.append(jax.ShapeDtypeStruct((n_tok, MOBA_W), F32))
        out_specs.append(pl.BlockSpec((tm, MOBA_W), row))
    return pl.pallas_call(
        functools.partial(_proj_kernel, tm=tm, seq_len=seq_len, pos_base=pos_base, block_sums=block_sums),
        grid=(n_tiles,),
        in_specs=[pl.BlockSpec((tm, d_model), row),
                  pl.BlockSpec(wg.shape, const), pl.BlockSpec(wl.shape, const), pl.BlockSpec(wgk2.shape, const),
                  pl.BlockSpec(bgk.shape, const), pl.BlockSpec(wm.shape, const), pl.BlockSpec(inv_lane.shape, const)],
        out_specs=out_specs,
        out_shape=out_shape,
        compiler_params=pltpu.CompilerParams(dimension_semantics=("arbitrary",), vmem_limit_bytes=VMEM_LIMIT_BYTES),
        name="proj",
    )(x2d, wg, wl, wgk2, bgk, wm, inv_lane)


def _topk_rows(gate, jrow, n_valid_limit):
    sel = jnp.zeros(gate.shape, F32)
    cur = gate
    for r in range(MOBA_TOPK):
        m = jnp.max(cur, axis=0, keepdims=True)
        idx = jnp.min(jnp.where(cur == m, jrow, 1e9), axis=0, keepdims=True)
        pick = jrow == idx
        counted = jnp.where(r < n_valid_limit, 1.0, 0.0)
        sel = jnp.where(pick, jnp.maximum(sel, counted), sel)
        cur = jnp.where(pick, -jnp.inf, cur)
    return sel


def _moba_prep_kernel(q_ref, k_ref, vt_ref, ksum_ref, qe_ref, ke_ref, vet_ref, *, nb):
    i = pl.program_id(1)
    ones_row = jnp.where(lax.broadcasted_iota(jnp.int32, (LANES - MOBA_DH, MOBA_BLOCK), 0) == 0, 1.0, 0.0)
    kmean = ksum_ref[0] * (1.0 / MOBA_BLOCK)
    lane = _lane_iota((1, LANES))
    jrow = lax.broadcasted_iota(jnp.int32, (nb, MOBA_BLOCK), 0).astype(F32)
    past = jrow < i.astype(F32)
    q_slots = []
    k_slots = []
    km_slots = []
    for ref_or_val, slots in ((q_ref[...], q_slots), (k_ref[...], k_slots), (kmean, km_slots)):
        e = _expand_heads64(ref_or_val)
        for h in range(MOBA_HEADS):
            slots.append(e[:, h * LANES:(h + 1) * LANES])
    for h in range(MOBA_HEADS):
        gate = lax.dot_general(km_slots[h], q_slots[h], NT_DIMS, precision=lax.Precision.HIGHEST,
                               preferred_element_type=F32)
        sel = _topk_rows(jnp.where(past, gate, -jnp.inf), jrow, i)
        notsel = jnp.where(past, 1.0 - sel, 0.0)
        pre = jnp.concatenate([jnp.zeros((FLAG_LANE0, MOBA_BLOCK), F32), notsel,
                               jnp.zeros((LANES - FLAG_LANE0 - nb, MOBA_BLOCK), F32)], axis=0)
        flags = pre.T
        sl = slice(h * LANES, (h + 1) * LANES)
        qe_ref[:, sl] = (q_slots[h] * LOG2E + flags).astype(BF16)
        ke_ref[:, sl] = jnp.where(lane == FLAG_LANE0 + i, -MASK_BIG, k_slots[h]).astype(BF16)
        vet_ref[0, sl, :] = jnp.concatenate([vt_ref[0, h * MOBA_DH:(h + 1) * MOBA_DH, :], ones_row], axis=0).astype(BF16)


def _moba_prep(mq, mk, mvt, ksum, *, batch, nb):
    n_tok = mq.shape[0]
    row = lambda b, i: (b * nb + i, 0)
    width = MOBA_HEADS * LANES
    ext = jax.ShapeDtypeStruct((n_tok, width), BF16)
    return pl.pallas_call(
        functools.partial(_moba_prep_kernel, nb=nb),
        grid=(batch, nb),
        in_specs=[pl.BlockSpec((MOBA_BLOCK, MOBA_W), row)] * 2
                 + [pl.BlockSpec((1, MOBA_W, MOBA_BLOCK), lambda b, i: (b, 0, i)),
                    pl.BlockSpec((1, nb, MOBA_W), lambda b, i: (b, 0, 0))],
        out_specs=[pl.BlockSpec((MOBA_BLOCK, width), row)] * 2
                  + [pl.BlockSpec((1, width, MOBA_BLOCK), lambda b, i: (b, 0, i))],
        out_shape=[ext, ext, jax.ShapeDtypeStruct((batch, width, nb * MOBA_BLOCK), BF16)],
        compiler_params=pltpu.CompilerParams(dimension_semantics=("arbitrary", "arbitrary"),
                                             vmem_limit_bytes=VMEM_LIMIT_BYTES),
        name="moba_prep",
    )(mq, mk, mvt, ksum)


def _moba_attn_kernel(gtab_ref, jtab_ref, kind_ref, q_ref, k_ref, vt_ref, o_ref, m_sc, acc_sc, *, tq):
    p = pl.program_id(1)
    g = gtab_ref[p]
    j = jtab_ref[p]
    kind = kind_ref[p]

    def scores(h):
        sl = slice(h * LANES, (h + 1) * LANES)
        return lax.dot_general(k_ref[:, sl], q_ref[:, sl], NT_DIMS, preferred_element_type=F32)

    def step(own):
        if own:
            first = (kind & 3) == 0
            keep = (j * MOBA_BLOCK + lax.broadcasted_iota(jnp.int32, (MOBA_BLOCK, tq), 0)
                    <= g * tq + lax.broadcasted_iota(jnp.int32, (MOBA_BLOCK, tq), 1))
        pending = [scores(h) for h in range(QK_LOOKAHEAD)]
        for h in range(MOBA_HEADS):
            s = pending.pop(0)
            if h + QK_LOOKAHEAD < MOBA_HEADS:
                pending.append(scores(h + QK_LOOKAHEAD))
            sl = slice(h * LANES, (h + 1) * LANES)
            m_old = m_sc[h:h + 1, :]
            acc_old = acc_sc[sl, :]
            if own:
                s = jnp.where(keep, s, -jnp.inf)
                m_old = jnp.where(first, -jnp.inf, m_old)
                acc_old = jnp.where(first, 0.0, acc_old)
            m_new = jnp.maximum(m_old, jnp.max(s, axis=0, keepdims=True))
            pexp = jnp.exp2(s - m_new)
            pv = jnp.dot(vt_ref[0, sl, :], pexp.astype(BF16), preferred_element_type=F32)
            acc_sc[sl, :] = jnp.exp2(m_old - m_new) * acc_old + pv
            m_sc[h:h + 1, :] = m_new

    @pl.when((kind & 3) != 2)
    def _():
        step(True)

    @pl.when((kind & 3) == 2)
    def _():
        step(False)

    @pl.when(kind >= 4)
    def _():
        slots = []
        for h in range(MOBA_HEADS):
            acc = acc_sc[h * LANES:(h + 1) * LANES, :]
            slots.append((acc / acc[FLAG_LANE0:FLAG_LANE0 + 1, :]).T)
        o_ref[...] = _compact_heads64(slots)


def _moba_attn(q_ext, k_ext, v_ext_t, *, batch, nb):
    n_tok = q_ext.shape[0]
    nq = ATTN_Q_BLOCKS
    assert nb % nq == 0
    tq = nq * MOBA_BLOCK
    n_tiles = nb // nq
    gs, js, kinds = [], [], []
    for g in range(n_tiles):
        order = [(g * nq + a, 0 if a == 0 else 1) for a in range(nq)] + [(j, 2) for j in range(g * nq)]
        for idx, (j, kind) in enumerate(order):
            gs.append(g)
            js.append(j)
            kinds.append(kind + (4 if idx == len(order) - 1 else 0))
    tabs = [jnp.asarray(t, jnp.int32) for t in (gs, js, kinds)]
    width = MOBA_HEADS * LANES
    grid_spec = pltpu.PrefetchScalarGridSpec(
        num_scalar_prefetch=3,
        grid=(batch, len(gs)),
        in_specs=[pl.BlockSpec((tq, width), lambda b, p, gt, jt, kt: (b * n_tiles + gt[p], 0)),
                  pl.BlockSpec((MOBA_BLOCK, width), lambda b, p, gt, jt, kt: (b * nb + jt[p], 0)),
                  pl.BlockSpec((1, width, MOBA_BLOCK), lambda b, p, gt, jt, kt: (b, 0, jt[p]))],
        out_specs=pl.BlockSpec((tq, MOBA_W), lambda b, p, gt, jt, kt: (b * n_tiles + gt[p], 0)),
        scratch_shapes=[pltpu.VMEM((MOBA_HEADS, tq), F32), pltpu.VMEM((width, tq), F32)],
    )
    return pl.pallas_call(
        functools.partial(_moba_attn_kernel, tq=tq),
        grid_spec=grid_spec,
        out_shape=jax.ShapeDtypeStruct((n_tok, MOBA_W), F32),
        compiler_params=pltpu.CompilerParams(dimension_semantics=("arbitrary", "arbitrary"),
                                             vmem_limit_bytes=VMEM_LIMIT_BYTES),
        name="moba_attn",
    )(*tabs, q_ext, k_ext, v_ext_t)


def _split3_bf16(x):
    x1 = x.astype(BF16)
    r1 = x - x1.astype(F32)
    x2 = r1.astype(BF16)
    x3 = (r1 - x2.astype(F32)).astype(BF16)
    return x1, x2, x3


def _gla_norm_gate(o, nw, gg):
    on = o * lax.rsqrt(jnp.mean(jnp.square(o), axis=-1, keepdims=True) + RMS_EPS) * nw
    return on * _silu(gg)


def _gla_kernel(q_ref, k_ref, lf_ref, v_ref, g_ref, nw_ref, o_ref, st_ref, st_sc, *, tt):
    t = pl.program_id(1)

    @pl.when(t == 0)
    def _():
        st_sc[...] = jnp.zeros(st_sc.shape, F32)

    c = GLA_CHUNK
    tril = lax.broadcasted_iota(jnp.int32, (c, c), 1) <= lax.broadcasted_iota(jnp.int32, (c, c), 0)
    ltri = jnp.where(tril, 1.0, 0.0).astype(BF16)
    nw = nw_ref[...]
    units = [(ci, h) for ci in range(tt // c) for h in range(GLA_HEADS)]
    where = {u: (slice(u[0] * c, (u[0] + 1) * c), slice(u[1] * LANES, (u[1] + 1) * LANES)) for u in units}

    cums = {}
    for u in units:
        g1, g2, g3 = _split3_bf16(lf_ref[where[u]])
        cums[u] = (jnp.dot(ltri, g1, preferred_element_type=F32) + jnp.dot(ltri, g2, preferred_element_type=F32)
                   + jnp.dot(ltri, g3, preferred_element_type=F32))
    qds, attn, upds, decays, intras = {}, {}, {}, {}, {}
    for u in units:
        b = cums[u]
        b_last = b[c - 1:c, :]
        q = q_ref[where[u]]
        k = k_ref[where[u]]
        qds[u] = (q * jnp.exp(b)).astype(BF16)
        kd = (k * jnp.exp(-b)).astype(BF16)
        attn[u] = lax.dot_general(qds[u], kd, NT_DIMS, preferred_element_type=F32)
        decays[u] = jnp.exp(b_last)
    for u in units:
        b = cums[u]
        kdec = (k_ref[where[u]] * jnp.exp(b[c - 1:c, :] - b)).astype(BF16)
        upds[u] = jnp.dot(v_ref[where[u]].T.astype(BF16), kdec, preferred_element_type=F32)
    for u in units:
        a = jnp.where(tril, attn[u], 0.0).astype(BF16)
        intras[u] = jnp.dot(a, v_ref[where[u]].astype(BF16), preferred_element_type=F32)
    state = [st_sc[h] for h in range(GLA_HEADS)]
    for u in units:
        h = u[1]
        o = lax.dot_general(qds[u], state[h].astype(BF16), NT_DIMS, preferred_element_type=F32) + intras[u]
        state[h] = state[h] * decays[u] + upds[u]
        o_ref[where[u]] = _gla_norm_gate(o, nw, g_ref[where[u]]).astype(o_ref.dtype)
    for h in range(GLA_HEADS):
        st_sc[h] = state[h]

    @pl.when(t == pl.num_programs(1) - 1)
    def _():
        for h in range(GLA_HEADS):
            st_ref[0, h] = st_sc[h].T[:GLA_DK, :]


def _gla_prompt(gq, gk, lf, gv, gg, norm_w, *, batch, seq, tt):
    n_tok = gq.shape[0]
    nt = seq // tt
    row = lambda b, t: (b * nt + t, 0)
    pad_w = GLA_HEADS * LANES
    return pl.pallas_call(
        functools.partial(_gla_kernel, tt=tt),
        grid=(batch, nt),
        in_specs=[pl.BlockSpec((tt, pad_w), row)] * 3 + [pl.BlockSpec((tt, GLA_V), row)] * 2
                 + [pl.BlockSpec((1, GLA_DV), lambda b, t: (0, 0))],
        out_specs=[pl.BlockSpec((tt, GLA_V), row),
                   pl.BlockSpec((1, GLA_HEADS, GLA_DK, GLA_DV), lambda b, t: (b, 0, 0, 0))],
        out_shape=[jax.ShapeDtypeStruct((n_tok, GLA_V), BF16),
                   jax.ShapeDtypeStruct((batch, GLA_HEADS, GLA_DK, GLA_DV), F32)],
        scratch_shapes=[pltpu.VMEM((GLA_HEADS, GLA_DV, LANES), F32)],
        compiler_params=pltpu.CompilerParams(dimension_semantics=("arbitrary", "arbitrary"),
                                             vmem_limit_bytes=VMEM_LIMIT_BYTES),
        name="gla_prompt",
    )(gq, gk, lf, gv, gg, norm_w)


def _gla_step_kernel(q_ref, k_ref, lf_ref, v_ref, g_ref, nw_ref, s_ref, o_ref, so_ref):
    eye = (lax.broadcasted_iota(jnp.int32, (GLA_DK, LANES), 0) == lax.broadcasted_iota(jnp.int32, (GLA_DK, LANES), 1))

    def column(rowvec):
        return jnp.sum(jnp.where(eye, rowvec, 0.0), axis=1, keepdims=True)

    nw = nw_ref[...]
    for h in range(GLA_HEADS):
        sl = slice(h * LANES, (h + 1) * LANES)
        decay = column(jnp.exp(lf_ref[0, :, sl]))
        kcol = column(k_ref[0, :, sl])
        qcol = column(q_ref[0, :, sl])
        s_new = decay * s_ref[0, h] + kcol * v_ref[0, :, sl]
        so_ref[0, h] = s_new
        o = jnp.sum(qcol * s_new, axis=0, keepdims=True)
        o_ref[0, :, sl] = _gla_norm_gate(o, nw, g_ref[0, :, sl]).astype(o_ref.dtype)


def _gla_step(gq, gk, lf, gv, gg, norm_w, state):
    n = gq.shape[0]
    pad_w = GLA_HEADS * LANES
    as3 = lambda a: a.reshape(n, 1, a.shape[-1])
    vec = lambda w: pl.BlockSpec((1, 1, w), lambda i: (i, 0, 0))
    st_spec = pl.BlockSpec((1, GLA_HEADS, GLA_DK, GLA_DV), lambda i: (i, 0, 0, 0))
    o, s_out = pl.pallas_call(
        _gla_step_kernel,
        grid=(n,),
        in_specs=[vec(pad_w)] * 3 + [vec(GLA_V)] * 2 + [pl.BlockSpec((1, GLA_DV), lambda i: (0, 0)), st_spec],
        out_specs=[vec(GLA_V), st_spec],
        out_shape=[jax.ShapeDtypeStruct((n, 1, GLA_V), BF16), jax.ShapeDtypeStruct(state.shape, F32)],
        compiler_params=pltpu.CompilerParams(dimension_semantics=("arbitrary",)),
        name="gla_step",
    )(as3(gq), as3(gk), as3(lf), as3(gv), as3(gg), norm_w, state)
    return o.reshape(n, GLA_V), s_out


ROUTE_PAGES_IN_FLIGHT = 16


def _sample_route_kernel(pt_ref, q_ref, ckt_hbm, top_ref, buf, sem, qb_sc, gp_sc,
                         *, layer, n_pages, page_size, nb_past):
    n = pl.program_id(0)
    n_seq = pl.num_programs(0)
    pages_per_block = MOBA_BLOCK // page_size
    blocks_in_flight = ROUTE_PAGES_IN_FLIGHT // pages_per_block

    def page_copy(seq, blk, a):
        slot = lax.rem(blk, blocks_in_flight) * pages_per_block + a
        phys = pt_ref[seq * n_pages + blk * pages_per_block + a]
        return pltpu.make_async_copy(ckt_hbm.at[layer, phys], buf.at[slot], sem.at[slot])

    @pl.when(n == 0)
    def _():
        for blk in range(blocks_in_flight):
            for a in range(pages_per_block):
                page_copy(n, jnp.int32(blk), a).start()

    eye = (lax.broadcasted_iota(jnp.int32, (MOBA_DH, MOBA_DH), 0)
           == lax.broadcasted_iota(jnp.int32, (MOBA_DH, MOBA_DH), 1))
    for h in range(MOBA_HEADS):
        qcol = jnp.sum(jnp.where(eye, q_ref[0, h:h + 1, :], 0.0), axis=1, keepdims=True)
        qb_sc[h] = jnp.broadcast_to(qcol, (MOBA_DH, page_size))

    def body(blk, carry):
        total = None
        for a in range(pages_per_block):
            page_copy(n, blk, a).wait()
            slot = lax.rem(blk, blocks_in_flight) * pages_per_block + a
            total = buf[slot] if total is None else total + buf[slot]
        gp_sc[blk] = jnp.sum(total * qb_sc[...], axis=1)
        nxt = blk + blocks_in_flight

        @pl.when(nxt < nb_past)
        def _():
            for a in range(pages_per_block):
                page_copy(n, nxt, a).start()

        @pl.when((nxt >= nb_past) & (n + 1 < n_seq))
        def _():
            for a in range(pages_per_block):
                page_copy(n + 1, nxt - nb_past, a).start()
        return carry

    lax.fori_loop(0, nb_past, body, 0)

    gate = jnp.sum(gp_sc[...], axis=-1, keepdims=True) * (1.0 / MOBA_BLOCK)
    jidx = lax.broadcasted_iota(jnp.int32, gate.shape, 0).astype(F32)
    lane = _lane_iota((MOBA_HEADS, LANES))
    out = jnp.zeros((MOBA_HEADS, LANES), jnp.int32)
    cur = gate
    for r in range(min(MOBA_TOPK, nb_past)):
        m = jnp.max(cur, axis=0, keepdims=True)
        idx = jnp.min(jnp.where(cur == m, jidx, 1e9), axis=0, keepdims=True)
        cur = jnp.where(jidx == idx, -jnp.inf, cur)
        out = jnp.where(lane == r, idx[0].astype(jnp.int32), out)
    top_ref[0] = out


def _sample_route(page_table_flat, q_heads, cache_kt, *, layer, nb_past):
    n = q_heads.shape[0]
    page_size = cache_kt.shape[4]
    n_pages = page_table_flat.shape[0] // n
    pages_per_block = MOBA_BLOCK // page_size
    assert nb_past % (ROUTE_PAGES_IN_FLIGHT // pages_per_block) == 0
    grid_spec = pltpu.PrefetchScalarGridSpec(
        num_scalar_prefetch=1,
        grid=(n,),
        in_specs=[pl.BlockSpec((1, MOBA_HEADS, MOBA_DH), lambda i, pt: (i, 0, 0)),
                  pl.BlockSpec(memory_space=pl.ANY)],
        out_specs=pl.BlockSpec((1, MOBA_HEADS, LANES), lambda i, pt: (i, 0, 0)),
        scratch_shapes=[pltpu.VMEM((ROUTE_PAGES_IN_FLIGHT, MOBA_HEADS, MOBA_DH, page_size), F32),
                        pltpu.SemaphoreType.DMA((ROUTE_PAGES_IN_FLIGHT,)),
                        pltpu.VMEM((MOBA_HEADS, MOBA_DH, page_size), F32),
                        pltpu.VMEM((nb_past, MOBA_HEADS, page_size), F32)],
    )
    return pl.pallas_call(
        functools.partial(_sample_route_kernel, layer=layer, n_pages=n_pages, page_size=page_size, nb_past=nb_past),
        grid_spec=grid_spec,
        out_shape=jax.ShapeDtypeStruct((n, MOBA_HEADS, LANES), jnp.int32),
        compiler_params=pltpu.CompilerParams(dimension_semantics=("arbitrary",)),
        name="sample_route",
    )(page_table_flat, q_heads, cache_kt)


def _sample_attn_kernel(top_ref, pt_ref, q_ref, kn_ref, vn_ref, ckt_hbm, cvt_hbm, o_ref, kbuf, vbuf, sem,
                        *, layer, n_pages, page_size, n_sel):
    n = pl.program_id(0)
    n_seq = pl.num_programs(0)
    pages_per_block = MOBA_BLOCK // page_size

    def copies(seq, h):
        half = lax.rem(seq, 2)
        out = []
        for r in range(n_sel):
            blk = top_ref[(seq * MOBA_HEADS + h) * n_sel + r]
            for a in range(pages_per_block):
                phys = pt_ref[seq * n_pages + blk * pages_per_block + a]
                dst = pl.ds((r * pages_per_block + a) * page_size, page_size)
                out.append(pltpu.make_async_copy(ckt_hbm.at[layer, phys, h], kbuf.at[half, h, :, dst],
                                                 sem.at[half, 0, h]))
                out.append(pltpu.make_async_copy(cvt_hbm.at[layer, phys, h], vbuf.at[half, h, :, dst],
                                                 sem.at[half, 1, h]))
        return out

    @pl.when(n == 0)
    def _():
        for h in range(MOBA_HEADS):
            for cp in copies(n, h):
                cp.start()

    @pl.when(n + 1 < n_seq)
    def _():
        for h in range(MOBA_HEADS):
            for cp in copies(n + 1, h):
                cp.start()

    half = lax.rem(n, 2)
    for h in range(MOBA_HEADS):
        for cp in copies(n, h):
            cp.wait()
        qh = q_ref[0, h:h + 1, :]
        qb = jnp.broadcast_to(qh, (SUBLANES, MOBA_DH)).astype(BF16)
        s = jnp.dot(qb, kbuf[half, h].astype(BF16), preferred_element_type=F32)
        s_own = jnp.sum(qh * kn_ref[0, h:h + 1, :], axis=-1, keepdims=True)
        m = jnp.maximum(jnp.max(s, axis=-1, keepdims=True), s_own)
        p = jnp.exp(s - m)
        p_own = jnp.exp(s_own - m)
        denom = jnp.sum(p, axis=-1, keepdims=True) + p_own
        o = (lax.dot_general(p.astype(BF16), vbuf[half, h].astype(BF16), NT_DIMS, preferred_element_type=F32)
             + p_own * vn_ref[0, h:h + 1, :])
        o_ref[0, h:h + 1, :] = (o / denom)[0:1, :]


def _sample_attn(top_flat, page_table_flat, q_heads, k_new, v_new, cache_kt, cache_vt, *, layer, n_sel):
    n = q_heads.shape[0]
    page_size = cache_kt.shape[4]
    n_pages = page_table_flat.shape[0] // n
    rows = n_sel * MOBA_BLOCK
    head_spec = pl.BlockSpec((1, MOBA_HEADS, MOBA_DH), lambda i, tp, pt: (i, 0, 0))
    grid_spec = pltpu.PrefetchScalarGridSpec(
        num_scalar_prefetch=2,
        grid=(n,),
        in_specs=[head_spec, head_spec, head_spec,
                  pl.BlockSpec(memory_space=pl.ANY), pl.BlockSpec(memory_space=pl.ANY)],
        out_specs=head_spec,
        scratch_shapes=[pltpu.VMEM((2, MOBA_HEADS, MOBA_DH, rows), F32), pltpu.VMEM((2, MOBA_HEADS, MOBA_DH, rows), F32),
                        pltpu.SemaphoreType.DMA((2, 2, MOBA_HEADS))],
    )
    return pl.pallas_call(
        functools.partial(_sample_attn_kernel, layer=layer, n_pages=n_pages, page_size=page_size, n_sel=n_sel),
        grid_spec=grid_spec,
        out_shape=jax.ShapeDtypeStruct((n, MOBA_HEADS, MOBA_DH), F32),
        compiler_params=pltpu.CompilerParams(dimension_semantics=("arbitrary",)),
        name="sample_attn",
    )(top_flat, page_table_flat, q_heads, k_new, v_new, cache_kt, cache_vt)


def _merge_kernel(x_ref, go_ref, mo_ref, mg_ref, w1_ref, w2_ref, g_ref, b_ref, y_ref, *, alpha):
    mo = (mo_ref[...] * _silu(mg_ref[...])).astype(BF16)
    z = (alpha * x_ref[...] + jnp.dot(go_ref[...], w1_ref[...], preferred_element_type=F32)
         + jnp.dot(mo, w2_ref[...], preferred_element_type=F32))
    mu = jnp.mean(z, axis=-1, keepdims=True)
    zc = z - mu
    var = jnp.mean(jnp.square(zc), axis=-1, keepdims=True)
    y_ref[...] = zc * lax.rsqrt(var + LN_EPS) * g_ref[...] + b_ref[...]


def _merge(x2d, gla_o, moba_o, mg, w1, w2, ln_g, ln_b, *, tm, alpha):
    n_tok, d_model = x2d.shape
    row = lambda i: (i, 0)
    const = lambda i: (0, 0)
    return pl.pallas_call(
        functools.partial(_merge_kernel, alpha=alpha),
        grid=(n_tok // tm,),
        in_specs=[pl.BlockSpec((tm, d_model), row), pl.BlockSpec((tm, GLA_V), row), pl.BlockSpec((tm, MOBA_W), row),
                  pl.BlockSpec((tm, MOBA_W), row), pl.BlockSpec(w1.shape, const), pl.BlockSpec(w2.shape, const),
                  pl.BlockSpec((1, d_model), const), pl.BlockSpec((1, d_model), const)],
        out_specs=pl.BlockSpec((tm, d_model), row),
        out_shape=jax.ShapeDtypeStruct((n_tok, d_model), F32),
        compiler_params=pltpu.CompilerParams(dimension_semantics=("arbitrary",), vmem_limit_bytes=VMEM_LIMIT_BYTES),
        name="merge",
    )(x2d, gla_o, moba_o, mg, w1, w2, ln_g, ln_b)


def _split_w_in(w_in_l, w_gk2_l, b_gk_l):
    n_gla = 2 * GLA_QK + 2 * GLA_V
    wg = w_in_l[:, :n_gla].astype(BF16)
    wl = jnp.pad(w_in_l[:, n_gla:n_gla + GLA_GATE_RANK], ((0, 0), (0, LANES - GLA_GATE_RANK))).astype(BF16)
    wm = w_in_l[:, n_gla + GLA_GATE_RANK:].astype(BF16)
    wgk2 = jnp.pad(w_gk2_l, ((0, LANES - GLA_GATE_RANK), (0, 0))).astype(BF16)
    return wg, wl, wgk2, b_gk_l.reshape(1, GLA_QK), wm


def kernel(x_prompt, x_sample, cache_k, cache_v, state_gla, page_table, w_in, w_gk2, b_gk, gla_norm_w, w_out,
           ln_g, ln_b):
    batch, seq, d_model = x_prompt.shape
    dec_batch, dec_seq, _ = x_sample.shape
    depth = w_in.shape[0]
    page_size = cache_k.shape[2]
    n_pages = page_table.shape[1]
    past_len = n_pages * page_size
    assert dec_seq == 1 and past_len % MOBA_BLOCK == 0 and MOBA_BLOCK % page_size == 0
    assert seq % MOBA_BLOCK == 0
    nb = seq // MOBA_BLOCK
    assert nb >= MOBA_TOPK and FLAG_LANE0 + nb <= LANES
    nb_past = past_len // MOBA_BLOCK
    n_sel = min(MOBA_TOPK, nb_past)
    alpha = (2.0 * depth) ** 0.25

    half = MOBA_DH // 2
    inv = ROPE_THETA ** (-jnp.arange(half, dtype=F32) / half)
    inv_lane = jnp.tile(inv, LANES // half).reshape(1, LANES)
    pt_flat = page_table.reshape(-1)
    cache_kt = jnp.transpose(cache_k, (0, 1, 3, 4, 2))
    cache_vt = jnp.transpose(cache_v, (0, 1, 3, 4, 2))

    xp = x_prompt.reshape(batch * seq, d_model)
    xs = x_sample.reshape(dec_batch * dec_seq, d_model)
    kp_l, vp_l, sp_l, ks_l, vs_l, ss_l = [], [], [], [], [], []
    for l in range(depth):
        w_parts = _split_w_in(w_in[l], w_gk2[l], b_gk[l])
        norm_w = gla_norm_w[l].reshape(1, GLA_DV)
        w1 = w_out[l, :GLA_V].astype(BF16)
        w2 = w_out[l, GLA_V:].astype(BF16)
        g_row = ln_g[l].reshape(1, d_model)
        b_row = ln_b[l].reshape(1, d_model)

        gq, gk, lf, gv, gg, mq, mk, mg, mkt, mvt, ksum = _project(
            xp, w_parts, inv_lane, tm=MOBA_BLOCK, seq_len=seq, pos_base=0, block_sums=True)
        ksum = ksum.reshape(batch, nb, SUBLANES, MOBA_W)[:, :, 0, :]
        q_ext, k_ext, v_ext_t = _moba_prep(mq, mk, mvt, ksum, batch=batch, nb=nb)
        moba_p = _moba_attn(q_ext, k_ext, v_ext_t, batch=batch, nb=nb)
        gla_p, st_p = _gla_prompt(gq, gk, lf, gv, gg, norm_w, batch=batch, seq=seq, tt=GLA_TOKENS_PER_STEP)
        kp_l.append(jnp.transpose(mkt.reshape(batch, MOBA_HEADS, MOBA_DH, seq), (0, 3, 1, 2)))
        vp_l.append(jnp.transpose(mvt.reshape(batch, MOBA_HEADS, MOBA_DH, seq), (0, 3, 1, 2)))
        sp_l.append(st_p)

        sq, sk, slf, sv, sg, smq, smk, smg, smkt, smvt, smv = _project(
            xs, w_parts, inv_lane, tm=dec_batch, seq_len=dec_seq, pos_base=past_len, block_sums=False)
        gla_s, st_s = _gla_step(sq, sk, slf, sv, sg, norm_w, state_gla[l])
        q_heads = smq.reshape(dec_batch, MOBA_HEADS, MOBA_DH)
        k_new = smk.reshape(dec_batch, MOBA_HEADS, MOBA_DH)
        v_new = smv.reshape(dec_batch, MOBA_HEADS, MOBA_DH)
        top = _sample_route(pt_flat, q_heads, cache_kt, layer=l, nb_past=nb_past)
        top_flat = top[:, :, :n_sel].reshape(-1)
        moba_s = _sample_attn(top_flat, pt_flat, q_heads, k_new, v_new, cache_kt, cache_vt, layer=l, n_sel=n_sel)
        moba_s = moba_s.reshape(dec_batch, MOBA_W)
        ks_l.append(jnp.transpose(smkt.reshape(MOBA_HEADS, MOBA_DH, dec_batch), (2, 0, 1))[:, None])
        vs_l.append(jnp.transpose(smvt.reshape(MOBA_HEADS, MOBA_DH, dec_batch), (2, 0, 1))[:, None])
        ss_l.append(st_s)

        xp = _merge(xp, gla_p, moba_p, mg, w1, w2, g_row, b_row, tm=2 * MOBA_BLOCK, alpha=alpha)
        xs = _merge(xs, gla_s, moba_s, smg, w1, w2, g_row, b_row, tm=dec_batch, alpha=alpha)

    return (xp.reshape(batch, seq, d_model), xs.reshape(dec_batch, dec_seq, d_model),
            jnp.stack(kp_l), jnp.stack(vp_l), jnp.stack(sp_l), jnp.stack(ks_l), jnp.stack(vs_l), jnp.stack(ss_l))
```

```python
import functools
import math

import jax
import jax.numpy as jnp
from jax import lax
from jax.experimental import pallas as pl
from jax.experimental.pallas import tpu as pltpu

GLA_HEADS = 4
GLA_DK = 64
GLA_DV = 128
GLA_QK = GLA_HEADS * GLA_DK
GLA_V = GLA_HEADS * GLA_DV
GLA_GATE_RANK = 16
GLA_GATE_NORM = 16.0
GLA_CHUNK = 64
MOBA_HEADS = 8
MOBA_DH = 64
MOBA_W = MOBA_HEADS * MOBA_DH
MOBA_BLOCK = 256
MOBA_TOPK = 3
ROPE_THETA = 10000.0
LN_EPS = 1e-5
RMS_EPS = 1e-6

LANES = 128
SUBLANES = 8
VMEM_LIMIT_BYTES = 48 * 1024 * 1024

MASK_BIG = 2.0 ** 100
FLAG_LANE0 = MOBA_DH
LOG2E = math.log2(math.e)
V_SLOT_ROWS = MOBA_DH + 16
QK_LOOKAHEAD = 8
GLA_TOKENS_PER_STEP = 512
ATTN_Q_BLOCKS = 2

F32 = jnp.float32
BF16 = jnp.bfloat16
NT_DIMS = (((1,), (1,)), ((), ()))


def _silu(x):
    return x / (1.0 + jnp.exp(-x))


def _log_sigmoid(z):
    return jnp.minimum(z, 0.0) - jnp.log(1.0 + jnp.exp(-jnp.abs(z)))


def _lane_iota(shape):
    return lax.broadcasted_iota(jnp.int32, shape, len(shape) - 1)


def _expand_heads64(x):
    rows, width = x.shape
    low = _lane_iota((1, LANES)) < MOBA_DH
    parts = []
    for g in range(width // LANES):
        a = x[:, g * LANES:(g + 1) * LANES]
        parts.append(jnp.where(low, a, 0.0))
        parts.append(jnp.where(low, pltpu.roll(a, LANES - MOBA_DH, 1), 0.0))
    return jnp.concatenate(parts, axis=1)


def _proj_kernel(x_ref, wg_ref, wl_ref, wgk2_ref, bgk_ref, wm_ref, inv_ref,
                 gq_ref, gk_ref, lf_ref, gv_ref, gg_ref, mq_ref, mk_ref, mg_ref, kt_ref, vt_ref, *opt_refs,
                 tm, seq_len, pos_base, block_sums):
    x = x_ref[...].astype(BF16)

    ug = jnp.dot(x, wg_ref[...], preferred_element_type=F32)
    gq_ref[...] = _expand_heads64(ug[:, :GLA_QK] * (GLA_DK ** -0.5))
    gk_ref[...] = _expand_heads64(ug[:, GLA_QK:2 * GLA_QK])
    gv_ref[...] = ug[:, 2 * GLA_QK:2 * GLA_QK + GLA_V]
    gg_ref[...] = ug[:, 2 * GLA_QK + GLA_V:]

    glow = jnp.dot(x, wl_ref[...], preferred_element_type=F32)
    z = jnp.dot(glow.astype(BF16), wgk2_ref[...], preferred_element_type=F32) + bgk_ref[...]
    lf_ref[...] = _expand_heads64(_log_sigmoid(z) / GLA_GATE_NORM)

    um = jnp.dot(x, wm_ref[...], preferred_element_type=F32)
    if seq_len % tm == 0:
        pos0 = pos_base + lax.rem(pl.program_id(0) * tm, seq_len)
        pos = (pos0 + lax.broadcasted_iota(jnp.int32, (tm, 1), 0)).astype(F32)
    else:
        assert seq_len == 1
        pos = jnp.full((tm, 1), pos_base, F32)
    ang = pos * inv_ref[...]
    cos = jnp.cos(ang)
    sin = jnp.sin(ang)
    first_half = (_lane_iota((1, LANES)) % MOBA_DH) < (MOBA_DH // 2)
    sin_signed = jnp.where(first_half, -sin, sin)

    def rope(u):
        parts = []
        for g in range(MOBA_W // LANES):
            a = u[:, g * LANES:(g + 1) * LANES]
            partner = jnp.where(first_half, pltpu.roll(a, LANES - MOBA_DH // 2, 1), pltpu.roll(a, MOBA_DH // 2, 1))
            parts.append(a * cos + partner * sin_signed)
        return jnp.concatenate(parts, axis=1)

    mq_ref[...] = rope(um[:, :MOBA_W]) * (MOBA_DH ** -0.5)
    mk = rope(um[:, MOBA_W:2 * MOBA_W])
    mv = um[:, 2 * MOBA_W:3 * MOBA_W]
    mk_ref[...] = mk
    mg_ref[...] = um[:, 3 * MOBA_W:]
    kt_ref[0] = mk.T
    vt_ref[0] = mv.T
    if block_sums:
        opt_refs[0][...] = jnp.broadcast_to(jnp.sum(mk, axis=0, keepdims=True), (1, SUBLANES, MOBA_W))
    else:
        opt_refs[0][...] = mv


def _project(x2d, w_parts, inv_lane, *, tm, seq_len, pos_base, block_sums):
    n_tok, d_model = x2d.shape
    wg, wl, wgk2, bgk, wm = w_parts
    n_tiles = n_tok // tm
    tiles_per_seq = max(seq_len // tm, 1)
    row = lambda i: (i, 0)
    const = lambda i: (0, 0)
    widths = (2 * GLA_QK, 2 * GLA_QK, 2 * GLA_QK, GLA_V, GLA_V, MOBA_W, MOBA_W, MOBA_W)
    out_shape = [jax.ShapeDtypeStruct((n_tok, w), F32) for w in widths]
    out_specs = [pl.BlockSpec((tm, w), row) for w in widths]
    for _ in range(2):
        out_shape.append(jax.ShapeDtypeStruct((n_tiles // tiles_per_seq, MOBA_W, tiles_per_seq * tm), F32))
        out_specs.append(pl.BlockSpec((1, MOBA_W, tm), lambda i: (i // tiles_per_seq, 0, i % tiles_per_seq)))
    if block_sums:
        out_shape.append(jax.ShapeDtypeStruct((n_tiles, SUBLANES, MOBA_W), F32))
        out_specs.append(pl.BlockSpec((1, SUBLANES, MOBA_W), lambda i: (i, 0, 0)))
    else:
        out_shape.append(jax.ShapeDtypeStruct((n_tok, MOBA_W), F32))
        out_specs.append(pl.BlockSpec((tm, MOBA_W), row))
    return pl.pallas_call(
        functools.partial(_proj_kernel, tm=tm, seq_len=seq_len, pos_base=pos_base, block_sums=block_sums),
        grid=(n_tiles,),
        in_specs=[pl.BlockSpec((tm, d_model), row),
                  pl.BlockSpec(wg.shape, const), pl.BlockSpec(wl.shape, const), pl.BlockSpec(wgk2.shape, const),
                  pl.BlockSpec(bgk.shape, const), pl.BlockSpec(wm.shape, const), pl.BlockSpec(inv_lane.shape, const)],
        out_specs=out_specs,
        out_shape=out_shape,
        compiler_params=pltpu.CompilerParams(dimension_semantics=("arbitrary",), vmem_limit_bytes=VMEM_LIMIT_BYTES),
        name="proj",
    )(x2d, wg, wl, wgk2, bgk, wm, inv_lane)


def _topk_rows(gate, jrow, n_valid_limit):
    sel = jnp.zeros(gate.shape, F32)
    cur = gate
    for r in range(MOBA_TOPK):
        m = jnp.max(cur, axis=0, keepdims=True)
        idx = jnp.min(jnp.where(cur == m, jrow, 1e9), axis=0, keepdims=True)
        pick = jrow == idx
        counted = jnp.where(r < n_valid_limit, 1.0, 0.0)
        sel = jnp.where(pick, jnp.maximum(sel, counted), sel)
        cur = jnp.where(pick, -jnp.inf, cur)
    return sel


def _moba_prep_kernel(q_ref, k_ref, vt_ref, ksum_ref, qet_ref, ke_ref, vet_ref, *, nb):
    i = pl.program_id(1)
    ones_row = jnp.where(lax.broadcasted_iota(jnp.int32, (V_SLOT_ROWS - MOBA_DH, MOBA_BLOCK), 0) == 0, 1.0, 0.0)
    kmean = ksum_ref[0] * (1.0 / MOBA_BLOCK)
    lane = _lane_iota((1, LANES))
    jrow = lax.broadcasted_iota(jnp.int32, (nb, MOBA_BLOCK), 0).astype(F32)
    past = jrow < i.astype(F32)
    q_slots = []
    k_slots = []
    km_slots = []
    for ref_or_val, slots in ((q_ref[...], q_slots), (k_ref[...], k_slots), (kmean, km_slots)):
        e = _expand_heads64(ref_or_val)
        for h in range(MOBA_HEADS):
            slots.append(e[:, h * LANES:(h + 1) * LANES])
    for h in range(MOBA_HEADS):
        km_hi = km_slots[h].astype(BF16)
        km_lo = (km_slots[h] - km_hi.astype(F32)).astype(BF16)
        q_hi = q_slots[h].astype(BF16)
        q_lo = (q_slots[h] - q_hi.astype(F32)).astype(BF16)
        gate = (lax.dot_general(km_hi, q_hi, NT_DIMS, preferred_element_type=F32)
                + lax.dot_general(km_hi, q_lo, NT_DIMS, preferred_element_type=F32)
                + lax.dot_general(km_lo, q_hi, NT_DIMS, preferred_element_type=F32))
        sel = _topk_rows(jnp.where(past, gate, -jnp.inf), jrow, i)
        notsel = jnp.where(past, 1.0 - sel, 0.0)
        pre = jnp.concatenate([jnp.zeros((FLAG_LANE0, MOBA_BLOCK), F32), notsel,
                               jnp.zeros((LANES - FLAG_LANE0 - nb, MOBA_BLOCK), F32)], axis=0)
        sl = slice(h * LANES, (h + 1) * LANES)
        qet_ref[0, sl, :] = ((q_slots[h] * LOG2E).T + pre).astype(BF16)
        ke_ref[:, sl] = jnp.where(lane == FLAG_LANE0 + i, -MASK_BIG, k_slots[h]).astype(BF16)
        vet_ref[0, h * V_SLOT_ROWS:(h + 1) * V_SLOT_ROWS, :] = jnp.concatenate(
            [vt_ref[0, h * MOBA_DH:(h + 1) * MOBA_DH, :], ones_row], axis=0).astype(BF16)


def _moba_prep(mq, mk, mvt, ksum, *, batch, nb):
    n_tok = mq.shape[0]
    row = lambda b, i: (b * nb + i, 0)
    width = MOBA_HEADS * LANES
    rows_spec = pl.BlockSpec((MOBA_BLOCK, width), row)
    cols_spec = pl.BlockSpec((1, width, MOBA_BLOCK), lambda b, i: (b, 0, i))
    rows_shape = jax.ShapeDtypeStruct((n_tok, width), BF16)
    cols_shape = jax.ShapeDtypeStruct((batch, width, nb * MOBA_BLOCK), BF16)
    return pl.pallas_call(
        functools.partial(_moba_prep_kernel, nb=nb),
        grid=(batch, nb),
        in_specs=[pl.BlockSpec((MOBA_BLOCK, MOBA_W), row)] * 2
                 + [pl.BlockSpec((1, MOBA_W, MOBA_BLOCK), lambda b, i: (b, 0, i)),
                    pl.BlockSpec((1, nb, MOBA_W), lambda b, i: (b, 0, 0))],
        out_specs=[cols_spec, rows_spec, pl.BlockSpec((1, MOBA_HEADS * V_SLOT_ROWS, MOBA_BLOCK), lambda b, i: (b, 0, i))],
        out_shape=[cols_shape, rows_shape,
                   jax.ShapeDtypeStruct((batch, MOBA_HEADS * V_SLOT_ROWS, nb * MOBA_BLOCK), BF16)],
        compiler_params=pltpu.CompilerParams(dimension_semantics=("arbitrary", "arbitrary"),
                                             vmem_limit_bytes=VMEM_LIMIT_BYTES),
        name="moba_prep",
    )(mq, mk, mvt, ksum)


def _moba_attn_kernel(gtab_ref, jtab_ref, kind_ref, qt_ref, k_ref, vt_ref, o_ref, m_sc, acc_sc, *, tq):
    p = pl.program_id(1)
    g = gtab_ref[p]
    j = jtab_ref[p]
    kind = kind_ref[p]

    def scores(h):
        sl = slice(h * LANES, (h + 1) * LANES)
        return jnp.dot(k_ref[:, sl], qt_ref[0, sl, :], preferred_element_type=F32)

    def step(own):
        if own:
            first = (kind & 3) == 0
            keep = (j * MOBA_BLOCK + lax.broadcasted_iota(jnp.int32, (MOBA_BLOCK, tq), 0)
                    <= g * tq + lax.broadcasted_iota(jnp.int32, (MOBA_BLOCK, tq), 1))
        pending = [scores(h) for h in range(QK_LOOKAHEAD)]
        for h in range(MOBA_HEADS):
            s = pending.pop(0)
            if h + QK_LOOKAHEAD < MOBA_HEADS:
                pending.append(scores(h + QK_LOOKAHEAD))
            sl = slice(h * LANES, (h + 1) * LANES)
            m_old = m_sc[h:h + 1, :]
            vsl = slice(h * V_SLOT_ROWS, (h + 1) * V_SLOT_ROWS)
            acc_old = acc_sc[vsl, :]
            if own:
                s = jnp.where(keep, s, -jnp.inf)
                m_old = jnp.where(first, -jnp.inf, m_old)
                acc_old = jnp.where(first, 0.0, acc_old)
            m_new = jnp.maximum(m_old, jnp.max(s, axis=0, keepdims=True))
            pexp = jnp.exp2(s - m_new)
            pv = jnp.dot(vt_ref[0, vsl, :], pexp.astype(BF16), preferred_element_type=F32)
            acc_sc[vsl, :] = jnp.exp2(m_old - m_new) * acc_old + pv
            m_sc[h:h + 1, :] = m_new

    @pl.when((kind & 3) != 2)
    def _():
        step(True)

    @pl.when((kind & 3) == 2)
    def _():
        step(False)

    @pl.when(kind >= 4)
    def _():
        outs = []
        for h in range(MOBA_HEADS):
            acc = acc_sc[h * V_SLOT_ROWS:(h + 1) * V_SLOT_ROWS, :]
            outs.append(acc[:MOBA_DH, :] / acc[MOBA_DH:MOBA_DH + 1, :])
        o_ref[...] = jnp.concatenate([jnp.concatenate(outs[2 * g:2 * g + 2], axis=0).T
                                      for g in range(MOBA_HEADS // 2)], axis=1)


def _moba_attn(q_ext_t, k_ext, v_ext_t, *, batch, nb):
    n_tok = k_ext.shape[0]
    nq = ATTN_Q_BLOCKS
    assert nb % nq == 0
    tq = nq * MOBA_BLOCK
    n_tiles = nb // nq
    gs, js, kinds = [], [], []
    for g in range(n_tiles):
        order = [(g * nq + a, 0 if a == 0 else 1) for a in range(nq)] + [(j, 2) for j in range(g * nq)]
        for idx, (j, kind) in enumerate(order):
            gs.append(g)
            js.append(j)
            kinds.append(kind + (4 if idx == len(order) - 1 else 0))
    tabs = [jnp.asarray(t, jnp.int32) for t in (gs, js, kinds)]
    width = MOBA_HEADS * LANES
    grid_spec = pltpu.PrefetchScalarGridSpec(
        num_scalar_prefetch=3,
        grid=(batch, len(gs)),
        in_specs=[pl.BlockSpec((1, width, tq), lambda b, p, gt, jt, kt: (b, 0, gt[p])),
                  pl.BlockSpec((MOBA_BLOCK, width), lambda b, p, gt, jt, kt: (b * nb + jt[p], 0)),
                  pl.BlockSpec((1, MOBA_HEADS * V_SLOT_ROWS, MOBA_BLOCK), lambda b, p, gt, jt, kt: (b, 0, jt[p]))],
        out_specs=pl.BlockSpec((tq, MOBA_W), lambda b, p, gt, jt, kt: (b * n_tiles + gt[p], 0)),
        scratch_shapes=[pltpu.VMEM((MOBA_HEADS, tq), F32), pltpu.VMEM((MOBA_HEADS * V_SLOT_ROWS, tq), F32)],
    )
    return pl.pallas_call(
        functools.partial(_moba_attn_kernel, tq=tq),
        grid_spec=grid_spec,
        out_shape=jax.ShapeDtypeStruct((n_tok, MOBA_W), F32),
        compiler_params=pltpu.CompilerParams(dimension_semantics=("arbitrary", "arbitrary"),
                                             vmem_limit_bytes=VMEM_LIMIT_BYTES),
        name="moba_attn",
    )(*tabs, q_ext_t, k_ext, v_ext_t)


def _split3_bf16(x):
    x1 = x.astype(BF16)
    r1 = x - x1.astype(F32)
    x2 = r1.astype(BF16)
    x3 = (r1 - x2.astype(F32)).astype(BF16)
    return x1, x2, x3


def _gla_norm_gate(o, nw, gg):
    on = o * lax.rsqrt(jnp.mean(jnp.square(o), axis=-1, keepdims=True) + RMS_EPS) * nw
    return on * _silu(gg)


def _gla_kernel(q_ref, k_ref, lf_ref, v_ref, g_ref, nw_ref, o_ref, st_ref, st_sc, *, tt):
    t = pl.program_id(1)

    @pl.when(t == 0)
    def _():
        st_sc[...] = jnp.zeros(st_sc.shape, F32)

    c = GLA_CHUNK
    tril = lax.broadcasted_iota(jnp.int32, (c, c), 1) <= lax.broadcasted_iota(jnp.int32, (c, c), 0)
    ltri = jnp.where(tril, 1.0, 0.0).astype(BF16)
    nw = nw_ref[...]
    units = [(ci, h) for ci in range(tt // c) for h in range(GLA_HEADS)]
    where = {u: (slice(u[0] * c, (u[0] + 1) * c), slice(u[1] * LANES, (u[1] + 1) * LANES)) for u in units}

    cums = {}
    for u in units:
        g1, g2, g3 = _split3_bf16(lf_ref[where[u]])
        cums[u] = (jnp.dot(ltri, g1, preferred_element_type=F32) + jnp.dot(ltri, g2, preferred_element_type=F32)
                   + jnp.dot(ltri, g3, preferred_element_type=F32))
    qds, attn, upds, decays, intras = {}, {}, {}, {}, {}
    for u in units:
        b = cums[u]
        b_last = b[c - 1:c, :]
        q = q_ref[where[u]]
        k = k_ref[where[u]]
        qds[u] = (q * jnp.exp(b)).astype(BF16)
        kd = (k * jnp.exp(-b)).astype(BF16)
        attn[u] = lax.dot_general(qds[u], kd, NT_DIMS, preferred_element_type=F32)
        decays[u] = jnp.exp(b_last)
    for u in units:
        b = cums[u]
        kdec = (k_ref[where[u]] * jnp.exp(b[c - 1:c, :] - b)).astype(BF16)
        upds[u] = jnp.dot(v_ref[where[u]].T.astype(BF16), kdec, preferred_element_type=F32)
    for u in units:
        a = jnp.where(tril, attn[u], 0.0).astype(BF16)
        intras[u] = jnp.dot(a, v_ref[where[u]].astype(BF16), preferred_element_type=F32)
    state = [st_sc[h] for h in range(GLA_HEADS)]
    for u in units:
        h = u[1]
        o = lax.dot_general(qds[u], state[h].astype(BF16), NT_DIMS, preferred_element_type=F32) + intras[u]
        state[h] = state[h] * decays[u] + upds[u]
        o_ref[where[u]] = _gla_norm_gate(o, nw, g_ref[where[u]]).astype(o_ref.dtype)
    for h in range(GLA_HEADS):
        st_sc[h] = state[h]

    @pl.when(t == pl.num_programs(1) - 1)
    def _():
        for h in range(GLA_HEADS):
            st_ref[0, h] = st_sc[h].T[:GLA_DK, :]


def _gla_prompt(gq, gk, lf, gv, gg, norm_w, *, batch, seq, tt):
    n_tok = gq.shape[0]
    nt = seq // tt
    row = lambda b, t: (b * nt + t, 0)
    pad_w = GLA_HEADS * LANES
    return pl.pallas_call(
        functools.partial(_gla_kernel, tt=tt),
        grid=(batch, nt),
        in_specs=[pl.BlockSpec((tt, pad_w), row)] * 3 + [pl.BlockSpec((tt, GLA_V), row)] * 2
                 + [pl.BlockSpec((1, GLA_DV), lambda b, t: (0, 0))],
        out_specs=[pl.BlockSpec((tt, GLA_V), row),
                   pl.BlockSpec((1, GLA_HEADS, GLA_DK, GLA_DV), lambda b, t: (b, 0, 0, 0))],
        out_shape=[jax.ShapeDtypeStruct((n_tok, GLA_V), BF16),
                   jax.ShapeDtypeStruct((batch, GLA_HEADS, GLA_DK, GLA_DV), F32)],
        scratch_shapes=[pltpu.VMEM((GLA_HEADS, GLA_DV, LANES), F32)],
        compiler_params=pltpu.CompilerParams(dimension_semantics=("arbitrary", "arbitrary"),
                                             vmem_limit_bytes=VMEM_LIMIT_BYTES),
        name="gla_prompt",
    )(gq, gk, lf, gv, gg, norm_w)


def _gla_step_kernel(q_ref, k_ref, lf_ref, v_ref, g_ref, nw_ref, s_ref, o_ref, so_ref):
    eye = (lax.broadcasted_iota(jnp.int32, (GLA_DK, LANES), 0) == lax.broadcasted_iota(jnp.int32, (GLA_DK, LANES), 1))

    def column(rowvec):
        return jnp.sum(jnp.where(eye, rowvec, 0.0), axis=1, keepdims=True)

    nw = nw_ref[...]
    for h in range(GLA_HEADS):
        sl = slice(h * LANES, (h + 1) * LANES)
        decay = column(jnp.exp(lf_ref[0, :, sl]))
        kcol = column(k_ref[0, :, sl])
        qcol = column(q_ref[0, :, sl])
        s_new = decay * s_ref[0, h] + kcol * v_ref[0, :, sl]
        so_ref[0, h] = s_new
        o = jnp.sum(qcol * s_new, axis=0, keepdims=True)
        o_ref[0, :, sl] = _gla_norm_gate(o, nw, g_ref[0, :, sl]).astype(o_ref.dtype)


def _gla_step(gq, gk, lf, gv, gg, norm_w, state):
    n = gq.shape[0]
    pad_w = GLA_HEADS * LANES
    as3 = lambda a: a.reshape(n, 1, a.shape[-1])
    vec = lambda w: pl.BlockSpec((1, 1, w), lambda i: (i, 0, 0))
    st_spec = pl.BlockSpec((1, GLA_HEADS, GLA_DK, GLA_DV), lambda i: (i, 0, 0, 0))
    o, s_out = pl.pallas_call(
        _gla_step_kernel,
        grid=(n,),
        in_specs=[vec(pad_w)] * 3 + [vec(GLA_V)] * 2 + [pl.BlockSpec((1, GLA_DV), lambda i: (0, 0)), st_spec],
        out_specs=[vec(GLA_V), st_spec],
        out_shape=[jax.ShapeDtypeStruct((n, 1, GLA_V), BF16), jax.ShapeDtypeStruct(state.shape, F32)],
        compiler_params=pltpu.CompilerParams(dimension_semantics=("arbitrary",)),
        name="gla_step",
    )(as3(gq), as3(gk), as3(lf), as3(gv), as3(gg), norm_w, state)
    return o.reshape(n, GLA_V), s_out


ROUTE_PAGES_IN_FLIGHT = 16


def _sample_route_kernel(pt_ref, q_ref, ckt_hbm, top_ref, buf, sem, qb_sc, gp_sc,
                         *, layer, n_pages, page_size, nb_past):
    n = pl.program_id(0)
    n_seq = pl.num_programs(0)
    pages_per_block = MOBA_BLOCK // page_size
    blocks_in_flight = ROUTE_PAGES_IN_FLIGHT // pages_per_block

    def page_copy(seq, blk, a):
        slot = lax.rem(blk, blocks_in_flight) * pages_per_block + a
        phys = pt_ref[seq * n_pages + blk * pages_per_block + a]
        return pltpu.make_async_copy(ckt_hbm.at[layer, phys], buf.at[slot], sem.at[slot])

    @pl.when(n == 0)
    def _():
        for blk in range(blocks_in_flight):
            for a in range(pages_per_block):
                page_copy(n, jnp.int32(blk), a).start()

    eye = (lax.broadcasted_iota(jnp.int32, (MOBA_DH, MOBA_DH), 0)
           == lax.broadcasted_iota(jnp.int32, (MOBA_DH, MOBA_DH), 1))
    for h in range(MOBA_HEADS):
        qcol = jnp.sum(jnp.where(eye, q_ref[0, h:h + 1, :], 0.0), axis=1, keepdims=True)
        qb_sc[h] = jnp.broadcast_to(qcol, (MOBA_DH, page_size))

    def body(blk, carry):
        total = None
        for a in range(pages_per_block):
            page_copy(n, blk, a).wait()
            slot = lax.rem(blk, blocks_in_flight) * pages_per_block + a
            total = buf[slot] if total is None else total + buf[slot]
        gp_sc[blk] = jnp.sum(total * qb_sc[...], axis=1)
        nxt = blk + blocks_in_flight

        @pl.when(nxt < nb_past)
        def _():
            for a in range(pages_per_block):
                page_copy(n, nxt, a).start()

        @pl.when((nxt >= nb_past) & (n + 1 < n_seq))
        def _():
            for a in range(pages_per_block):
                page_copy(n + 1, nxt - nb_past, a).start()
        return carry

    lax.fori_loop(0, nb_past, body, 0)

    gate = jnp.sum(gp_sc[...], axis=-1, keepdims=True) * (1.0 / MOBA_BLOCK)
    jidx = lax.broadcasted_iota(jnp.int32, gate.shape, 0).astype(F32)
    lane = _lane_iota((MOBA_HEADS, LANES))
    out = jnp.zeros((MOBA_HEADS, LANES), jnp.int32)
    cur = gate
    for r in range(min(MOBA_TOPK, nb_past)):
        m = jnp.max(cur, axis=0, keepdims=True)
        idx = jnp.min(jnp.where(cur == m, jidx, 1e9), axis=0, keepdims=True)
        cur = jnp.where(jidx == idx, -jnp.inf, cur)
        out = jnp.where(lane == r, idx[0].astype(jnp.int32), out)
    top_ref[0] = out


def _sample_route(page_table_flat, q_heads, cache_kt, *, layer, nb_past):
    n = q_heads.shape[0]
    page_size = cache_kt.shape[4]
    n_pages = page_table_flat.shape[0] // n
    pages_per_block = MOBA_BLOCK // page_size
    assert nb_past % (ROUTE_PAGES_IN_FLIGHT // pages_per_block) == 0
    grid_spec = pltpu.PrefetchScalarGridSpec(
        num_scalar_prefetch=1,
        grid=(n,),
        in_specs=[pl.BlockSpec((1, MOBA_HEADS, MOBA_DH), lambda i, pt: (i, 0, 0)),
                  pl.BlockSpec(memory_space=pl.ANY)],
        out_specs=pl.BlockSpec((1, MOBA_HEADS, LANES), lambda i, pt: (i, 0, 0)),
        scratch_shapes=[pltpu.VMEM((ROUTE_PAGES_IN_FLIGHT, MOBA_HEADS, MOBA_DH, page_size), F32),
                        pltpu.SemaphoreType.DMA((ROUTE_PAGES_IN_FLIGHT,)),
                        pltpu.VMEM((MOBA_HEADS, MOBA_DH, page_size), F32),
                        pltpu.VMEM((nb_past, MOBA_HEADS, page_size), F32)],
    )
    return pl.pallas_call(
        functools.partial(_sample_route_kernel, layer=layer, n_pages=n_pages, page_size=page_size, nb_past=nb_past),
        grid_spec=grid_spec,
        out_shape=jax.ShapeDtypeStruct((n, MOBA_HEADS, LANES), jnp.int32),
        compiler_params=pltpu.CompilerParams(dimension_semantics=("arbitrary",)),
        name="sample_route",
    )(page_table_flat, q_heads, cache_kt)


def _sample_attn_kernel(top_ref, pt_ref, q_ref, kn_ref, vn_ref, ckt_hbm, cvt_hbm, o_ref, kbuf, vbuf, sem,
                        *, layer, n_pages, page_size, n_sel):
    n = pl.program_id(0)
    n_seq = pl.num_programs(0)
    pages_per_block = MOBA_BLOCK // page_size

    def copies(seq, h):
        half = lax.rem(seq, 2)
        out = []
        for r in range(n_sel):
            blk = top_ref[(seq * MOBA_HEADS + h) * n_sel + r]
            for a in range(pages_per_block):
                phys = pt_ref[seq * n_pages + blk * pages_per_block + a]
                dst = pl.ds((r * pages_per_block + a) * page_size, page_size)
                out.append(pltpu.make_async_copy(ckt_hbm.at[layer, phys, h], kbuf.at[half, h, :, dst],
                                                 sem.at[half, 0, h]))
                out.append(pltpu.make_async_copy(cvt_hbm.at[layer, phys, h], vbuf.at[half, h, :, dst],
                                                 sem.at[half, 1, h]))
        return out

    @pl.when(n == 0)
    def _():
        for h in range(MOBA_HEADS):
            for cp in copies(n, h):
                cp.start()

    @pl.when(n + 1 < n_seq)
    def _():
        for h in range(MOBA_HEADS):
            for cp in copies(n + 1, h):
                cp.start()

    half = lax.rem(n, 2)
    for h in range(MOBA_HEADS):
        for cp in copies(n, h):
            cp.wait()
    qs = [q_ref[0, h:h + 1, :] for h in range(MOBA_HEADS)]
    scores = [jnp.dot(jnp.broadcast_to(qs[h], (SUBLANES, MOBA_DH)).astype(BF16), kbuf[half, h].astype(BF16),
                      preferred_element_type=F32) for h in range(MOBA_HEADS)]
    probs, p_owns, denoms = [], [], []
    for h in range(MOBA_HEADS):
        s_own = jnp.sum(qs[h] * kn_ref[0, h:h + 1, :], axis=-1, keepdims=True)
        m = jnp.maximum(jnp.max(scores[h], axis=-1, keepdims=True), s_own)
        p = jnp.exp(scores[h] - m)
        p_own = jnp.exp(s_own - m)
        probs.append(p.astype(BF16))
        p_owns.append(p_own)
        denoms.append(jnp.sum(p, axis=-1, keepdims=True) + p_own)
    outs = [lax.dot_general(probs[h], vbuf[half, h].astype(BF16), NT_DIMS, preferred_element_type=F32)
            for h in range(MOBA_HEADS)]
    for h in range(MOBA_HEADS):
        o = outs[h] + p_owns[h] * vn_ref[0, h:h + 1, :]
        o_ref[0, h:h + 1, :] = (o / denoms[h])[0:1, :]


def _sample_attn(top_flat, page_table_flat, q_heads, k_new, v_new, cache_kt, cache_vt, *, layer, n_sel):
    n = q_heads.shape[0]
    page_size = cache_kt.shape[4]
    n_pages = page_table_flat.shape[0] // n
    rows = n_sel * MOBA_BLOCK
    head_spec = pl.BlockSpec((1, MOBA_HEADS, MOBA_DH), lambda i, tp, pt: (i, 0, 0))
    grid_spec = pltpu.PrefetchScalarGridSpec(
        num_scalar_prefetch=2,
        grid=(n,),
        in_specs=[head_spec, head_spec, head_spec,
                  pl.BlockSpec(memory_space=pl.ANY), pl.BlockSpec(memory_space=pl.ANY)],
        out_specs=head_spec,
        scratch_shapes=[pltpu.VMEM((2, MOBA_HEADS, MOBA_DH, rows), F32), pltpu.VMEM((2, MOBA_HEADS, MOBA_DH, rows), F32),
                        pltpu.SemaphoreType.DMA((2, 2, MOBA_HEADS))],
    )
    return pl.pallas_call(
        functools.partial(_sample_attn_kernel, layer=layer, n_pages=n_pages, page_size=page_size, n_sel=n_sel),
        grid_spec=grid_spec,
        out_shape=jax.ShapeDtypeStruct((n, MOBA_HEADS, MOBA_DH), F32),
        compiler_params=pltpu.CompilerParams(dimension_semantics=("arbitrary",)),
        name="sample_attn",
    )(top_flat, page_table_flat, q_heads, k_new, v_new, cache_kt, cache_vt)


def _merge_kernel(x_ref, go_ref, mo_ref, mg_ref, w1_ref, w2_ref, g_ref, b_ref, y_ref, *, alpha):
    mo = (mo_ref[...] * _silu(mg_ref[...])).astype(BF16)
    z = (alpha * x_ref[...] + jnp.dot(go_ref[...], w1_ref[...], preferred_element_type=F32)
         + jnp.dot(mo, w2_ref[...], preferred_element_type=F32))
    mu = jnp.mean(z, axis=-1, keepdims=True)
    zc = z - mu
    var = jnp.mean(jnp.square(zc), axis=-1, keepdims=True)
    y_ref[...] = zc * lax.rsqrt(var + LN_EPS) * g_ref[...] + b_ref[...]


def _merge(x2d, gla_o, moba_o, mg, w1, w2, ln_g, ln_b, *, tm, alpha):
    n_tok, d_model = x2d.shape
    row = lambda i: (i, 0)
    const = lambda i: (0, 0)
    return pl.pallas_call(
        functools.partial(_merge_kernel, alpha=alpha),
        grid=(n_tok // tm,),
        in_specs=[pl.BlockSpec((tm, d_model), row), pl.BlockSpec((tm, GLA_V), row), pl.BlockSpec((tm, MOBA_W), row),
                  pl.BlockSpec((tm, MOBA_W), row), pl.BlockSpec(w1.shape, const), pl.BlockSpec(w2.shape, const),
                  pl.BlockSpec((1, d_model), const), pl.BlockSpec((1, d_model), const)],
        out_specs=pl.BlockSpec((tm, d_model), row),
        out_shape=jax.ShapeDtypeStruct((n_tok, d_model), F32),
        compiler_params=pltpu.CompilerParams(dimension_semantics=("arbitrary",), vmem_limit_bytes=VMEM_LIMIT_BYTES),
        name="merge",
    )(x2d, gla_o, moba_o, mg, w1, w2, ln_g, ln_b)


def _split_w_in(w_in_l, w_gk2_l, b_gk_l):
    n_gla = 2 * GLA_QK + 2 * GLA_V
    wg = w_in_l[:, :n_gla].astype(BF16)
    wl = jnp.pad(w_in_l[:, n_gla:n_gla + GLA_GATE_RANK], ((0, 0), (0, LANES - GLA_GATE_RANK))).astype(BF16)
    wm = w_in_l[:, n_gla + GLA_GATE_RANK:].astype(BF16)
    wgk2 = jnp.pad(w_gk2_l, ((0, LANES - GLA_GATE_RANK), (0, 0))).astype(BF16)
    return wg, wl, wgk2, b_gk_l.reshape(1, GLA_QK), wm


def kernel(x_prompt, x_sample, cache_k, cache_v, state_gla, page_table, w_in, w_gk2, b_gk, gla_norm_w, w_out,
           ln_g, ln_b):
    batch, seq, d_model = x_prompt.shape
    dec_batch, dec_seq, _ = x_sample.shape
    depth = w_in.shape[0]
    page_size = cache_k.shape[2]
    n_pages = page_table.shape[1]
    past_len = n_pages * page_size
    assert dec_seq == 1 and past_len % MOBA_BLOCK == 0 and MOBA_BLOCK % page_size == 0
    assert seq % MOBA_BLOCK == 0
    nb = seq // MOBA_BLOCK
    assert nb >= MOBA_TOPK and FLAG_LANE0 + nb <= LANES
    nb_past = past_len // MOBA_BLOCK
    n_sel = min(MOBA_TOPK, nb_past)
    alpha = (2.0 * depth) ** 0.25

    half = MOBA_DH // 2
    inv = ROPE_THETA ** (-jnp.arange(half, dtype=F32) / half)
    inv_lane = jnp.tile(inv, LANES // half).reshape(1, LANES)
    pt_flat = page_table.reshape(-1)
    cache_kt = jnp.transpose(cache_k, (0, 1, 3, 4, 2))
    cache_vt = jnp.transpose(cache_v, (0, 1, 3, 4, 2))

    xp = x_prompt.reshape(batch * seq, d_model)
    xs = x_sample.reshape(dec_batch * dec_seq, d_model)
    kp_l, vp_l, sp_l, ks_l, vs_l, ss_l = [], [], [], [], [], []
    for l in range(depth):
        w_parts = _split_w_in(w_in[l], w_gk2[l], b_gk[l])
        norm_w = gla_norm_w[l].reshape(1, GLA_DV)
        w1 = w_out[l, :GLA_V].astype(BF16)
        w2 = w_out[l, GLA_V:].astype(BF16)
        g_row = ln_g[l].reshape(1, d_model)
        b_row = ln_b[l].reshape(1, d_model)

        gq, gk, lf, gv, gg, mq, mk, mg, mkt, mvt, ksum = _project(
            xp, w_parts, inv_lane, tm=MOBA_BLOCK, seq_len=seq, pos_base=0, block_sums=True)
        ksum = ksum.reshape(batch, nb, SUBLANES, MOBA_W)[:, :, 0, :]
        q_ext_t, k_ext, v_ext_t = _moba_prep(mq, mk, mvt, ksum, batch=batch, nb=nb)
        moba_p = _moba_attn(q_ext_t, k_ext, v_ext_t, batch=batch, nb=nb)
        gla_p, st_p = _gla_prompt(gq, gk, lf, gv, gg, norm_w, batch=batch, seq=seq, tt=GLA_TOKENS_PER_STEP)
        kp_l.append(jnp.transpose(mkt.reshape(batch, MOBA_HEADS, MOBA_DH, seq), (0, 3, 1, 2)))
        vp_l.append(jnp.transpose(mvt.reshape(batch, MOBA_HEADS, MOBA_DH, seq), (0, 3, 1, 2)))
        sp_l.append(st_p)

        sq, sk, slf, sv, sg, smq, smk, smg, smkt, smvt, smv = _project(
            xs, w_parts, inv_lane, tm=dec_batch, seq_len=dec_seq, pos_base=past_len, block_sums=False)
        gla_s, st_s = _gla_step(sq, sk, slf, sv, sg, norm_w, state_gla[l])
        q_heads = smq.reshape(dec_batch, MOBA_HEADS, MOBA_DH)
        k_new = smk.reshape(dec_batch, MOBA_HEADS, MOBA_DH)
        v_new = smv.reshape(dec_batch, MOBA_HEADS, MOBA_DH)
        top = _sample_route(pt_flat, q_heads, cache_kt, layer=l, nb_past=nb_past)
        top_flat = top[:, :, :n_sel].reshape(-1)
        moba_s = _sample_attn(top_flat, pt_flat, q_heads, k_new, v_new, cache_kt, cache_vt, layer=l, n_sel=n_sel)
        moba_s = moba_s.reshape(dec_batch, MOBA_W)
        ks_l.append(jnp.transpose(smkt.reshape(MOBA_HEADS, MOBA_DH, dec_batch), (2, 0, 1))[:, None])
        vs_l.append(jnp.transpose(smvt.reshape(MOBA_HEADS, MOBA_DH, dec_batch), (2, 0, 1))[:, None])
        ss_l.append(st_s)

        xp = _merge(xp, gla_p, moba_p, mg, w1, w2, g_row, b_row, tm=2 * MOBA_BLOCK, alpha=alpha)
        xs = _merge(xs, gla_s, moba_s, smg, w1, w2, g_row, b_row, tm=dec_batch, alpha=alpha)

    return (xp.reshape(batch, seq, d_model), xs.reshape(dec_batch, dec_seq, d_model),
            jnp.stack(kp_l), jnp.stack(vp_l), jnp.stack(sp_l), jnp.stack(ks_l), jnp.stack(vs_l), jnp.stack(ss_l))
```

```python
import functools
import math

import jax
import jax.numpy as jnp
from jax import lax
from jax.experimental import pallas as pl
from jax.experimental.pallas import tpu as pltpu

GLA_HEADS = 4
GLA_DK = 64
GLA_DV = 128
GLA_QK = GLA_HEADS * GLA_DK
GLA_V = GLA_HEADS * GLA_DV
GLA_GATE_RANK = 16
GLA_GATE_NORM = 16.0
GLA_CHUNK = 64
MOBA_HEADS = 8
MOBA_DH = 64
MOBA_W = MOBA_HEADS * MOBA_DH
MOBA_BLOCK = 256
MOBA_TOPK = 3
ROPE_THETA = 10000.0
LN_EPS = 1e-5
RMS_EPS = 1e-6

LANES = 128
SUBLANES = 8
VMEM_LIMIT_BYTES = 48 * 1024 * 1024

MASK_BIG = 2.0 ** 100
FLAG_LANE0 = MOBA_DH
LOG2E = math.log2(math.e)
V_SLOT_ROWS = MOBA_DH + 16
QK_LOOKAHEAD = 8
GLA_TOKENS_PER_STEP = 512
ATTN_TILE_BLOCKS = 2

F32 = jnp.float32
BF16 = jnp.bfloat16
NT_DIMS = (((1,), (1,)), ((), ()))


def _silu(x):
    return x / (1.0 + jnp.exp(-x))


def _log_sigmoid(z):
    return jnp.minimum(z, 0.0) - jnp.log(1.0 + jnp.exp(-jnp.abs(z)))


def _lane_iota(shape):
    return lax.broadcasted_iota(jnp.int32, shape, len(shape) - 1)


def _expand_heads64(x):
    rows, width = x.shape
    low = _lane_iota((1, LANES)) < MOBA_DH
    parts = []
    for g in range(width // LANES):
        a = x[:, g * LANES:(g + 1) * LANES]
        parts.append(jnp.where(low, a, 0.0))
        parts.append(jnp.where(low, pltpu.roll(a, LANES - MOBA_DH, 1), 0.0))
    return jnp.concatenate(parts, axis=1)


def _proj_kernel(x_ref, wg_ref, wl_ref, wgk2_ref, bgk_ref, wm_ref, inv_ref,
                 gq_ref, gk_ref, lf_ref, gv_ref, gg_ref, mq_ref, mk_ref, mg_ref, kt_ref, vt_ref, *opt_refs,
                 tm, seq_len, pos_base, block_sums):
    x = x_ref[...].astype(BF16)

    ug = jnp.dot(x, wg_ref[...], preferred_element_type=F32)
    gq_ref[...] = _expand_heads64(ug[:, :GLA_QK] * (GLA_DK ** -0.5))
    gk_ref[...] = _expand_heads64(ug[:, GLA_QK:2 * GLA_QK])
    gv_ref[...] = ug[:, 2 * GLA_QK:2 * GLA_QK + GLA_V]
    gg_ref[...] = ug[:, 2 * GLA_QK + GLA_V:]

    glow = jnp.dot(x, wl_ref[...], preferred_element_type=F32)
    z = jnp.dot(glow.astype(BF16), wgk2_ref[...], preferred_element_type=F32) + bgk_ref[...]
    lf_ref[...] = _expand_heads64(_log_sigmoid(z) / GLA_GATE_NORM)

    um = jnp.dot(x, wm_ref[...], preferred_element_type=F32)
    if seq_len % tm == 0:
        pos0 = pos_base + lax.rem(pl.program_id(0) * tm, seq_len)
        pos = (pos0 + lax.broadcasted_iota(jnp.int32, (tm, 1), 0)).astype(F32)
    else:
        assert seq_len == 1
        pos = jnp.full((tm, 1), pos_base, F32)
    ang = pos * inv_ref[...]
    cos = jnp.cos(ang)
    sin = jnp.sin(ang)
    first_half = (_lane_iota((1, LANES)) % MOBA_DH) < (MOBA_DH // 2)
    sin_signed = jnp.where(first_half, -sin, sin)

    def rope(u):
        parts = []
        for g in range(MOBA_W // LANES):
            a = u[:, g * LANES:(g + 1) * LANES]
            partner = jnp.where(first_half, pltpu.roll(a, LANES - MOBA_DH // 2, 1), pltpu.roll(a, MOBA_DH // 2, 1))
            parts.append(a * cos + partner * sin_signed)
        return jnp.concatenate(parts, axis=1)

    mq_ref[...] = rope(um[:, :MOBA_W]) * (MOBA_DH ** -0.5)
    mk = rope(um[:, MOBA_W:2 * MOBA_W])
    mv = um[:, 2 * MOBA_W:3 * MOBA_W]
    mk_ref[...] = mk
    mg_ref[...] = um[:, 3 * MOBA_W:]
    kt_ref[0] = mk.T
    vt_ref[0] = mv.T
    if block_sums:
        opt_refs[0][...] = jnp.broadcast_to(jnp.sum(mk, axis=0, keepdims=True), (1, SUBLANES, MOBA_W))
    else:
        opt_refs[0][...] = mv


def _project(x2d, w_parts, inv_lane, *, tm, seq_len, pos_base, block_sums):
    n_tok, d_model = x2d.shape
    wg, wl, wgk2, bgk, wm = w_parts
    n_tiles = n_tok // tm
    tiles_per_seq = max(seq_len // tm, 1)
    row = lambda i: (i, 0)
    const = lambda i: (0, 0)
    widths = (2 * GLA_QK, 2 * GLA_QK, 2 * GLA_QK, GLA_V, GLA_V, MOBA_W, MOBA_W, MOBA_W)
    out_shape = [jax.ShapeDtypeStruct((n_tok, w), F32) for w in widths]
    out_specs = [pl.BlockSpec((tm, w), row) for w in widths]
    for _ in range(2):
        out_shape.append(jax.ShapeDtypeStruct((n_tiles // tiles_per_seq, MOBA_W, tiles_per_seq * tm), F32))
        out_specs.append(pl.BlockSpec((1, MOBA_W, tm), lambda i: (i // tiles_per_seq, 0, i % tiles_per_seq)))
    if block_sums:
        out_shape.append(jax.ShapeDtypeStruct((n_tiles, SUBLANES, MOBA_W), F32))
        out_specs.append(pl.BlockSpec((1, SUBLANES, MOBA_W), lambda i: (i, 0, 0)))
    else:
        out_shape.append(jax.ShapeDtypeStruct((n_tok, MOBA_W), F32))
        out_specs.append(pl.BlockSpec((tm, MOBA_W), row))
    return pl.pallas_call(
        functools.partial(_proj_kernel, tm=tm, seq_len=seq_len, pos_base=pos_base, block_sums=block_sums),
        grid=(n_tiles,),
        in_specs=[pl.BlockSpec((tm, d_model), row),
                  pl.BlockSpec(wg.shape, const), pl.BlockSpec(wl.shape, const), pl.BlockSpec(wgk2.shape, const),
                  pl.BlockSpec(bgk.shape, const), pl.BlockSpec(wm.shape, const), pl.BlockSpec(inv_lane.shape, const)],
        out_specs=out_specs,
        out_shape=out_shape,
        compiler_params=pltpu.CompilerParams(dimension_semantics=("arbitrary",), vmem_limit_bytes=VMEM_LIMIT_BYTES),
        name="proj",
    )(x2d, wg, wl, wgk2, bgk, wm, inv_lane)


def _topk_rows(gate, jrow, n_valid_limit):
    sel = jnp.zeros(gate.shape, F32)
    cur = gate
    for r in range(MOBA_TOPK):
        m = jnp.max(cur, axis=0, keepdims=True)
        idx = jnp.min(jnp.where(cur == m, jrow, 1e9), axis=0, keepdims=True)
        pick = jrow == idx
        counted = jnp.where(r < n_valid_limit, 1.0, 0.0)
        sel = jnp.where(pick, jnp.maximum(sel, counted), sel)
        cur = jnp.where(pick, -jnp.inf, cur)
    return sel


def _moba_prep_kernel(q_ref, k_ref, vt_ref, ksum_ref, qet_ref, ke_ref, vet_ref, *, nb):
    i = pl.program_id(1)
    ones_row = jnp.where(lax.broadcasted_iota(jnp.int32, (V_SLOT_ROWS - MOBA_DH, MOBA_BLOCK), 0) == 0, 1.0, 0.0)
    kmean = ksum_ref[0] * (1.0 / MOBA_BLOCK)
    lane = _lane_iota((1, LANES))
    jrow = lax.broadcasted_iota(jnp.int32, (nb, MOBA_BLOCK), 0).astype(F32)
    past = jrow < i.astype(F32)
    q_slots = []
    k_slots = []
    km_slots = []
    for ref_or_val, slots in ((q_ref[...], q_slots), (k_ref[...], k_slots), (kmean, km_slots)):
        e = _expand_heads64(ref_or_val)
        for h in range(MOBA_HEADS):
            slots.append(e[:, h * LANES:(h + 1) * LANES])
    for h in range(MOBA_HEADS):
        km_hi = km_slots[h].astype(BF16)
        km_lo = (km_slots[h] - km_hi.astype(F32)).astype(BF16)
        q_hi = q_slots[h].astype(BF16)
        q_lo = (q_slots[h] - q_hi.astype(F32)).astype(BF16)
        gate = (lax.dot_general(km_hi, q_hi, NT_DIMS, preferred_element_type=F32)
                + lax.dot_general(km_hi, q_lo, NT_DIMS, preferred_element_type=F32)
                + lax.dot_general(km_lo, q_hi, NT_DIMS, preferred_element_type=F32))
        sel = _topk_rows(jnp.where(past, gate, -jnp.inf), jrow, i)
        notsel = jnp.where(past, 1.0 - sel, 0.0)
        pre = jnp.concatenate([jnp.zeros((FLAG_LANE0, MOBA_BLOCK), F32), notsel,
                               jnp.zeros((LANES - FLAG_LANE0 - nb, MOBA_BLOCK), F32)], axis=0)
        sl = slice(h * LANES, (h + 1) * LANES)
        qet_ref[0, sl, :] = ((q_slots[h] * LOG2E).T + pre).astype(BF16)
        ke_ref[:, sl] = jnp.where(lane == FLAG_LANE0 + i, -MASK_BIG, k_slots[h]).astype(BF16)
        vet_ref[0, h * V_SLOT_ROWS:(h + 1) * V_SLOT_ROWS, :] = jnp.concatenate(
            [vt_ref[0, h * MOBA_DH:(h + 1) * MOBA_DH, :], ones_row], axis=0).astype(BF16)


def _moba_prep(mq, mk, mvt, ksum, *, batch, nb):
    n_tok = mq.shape[0]
    row = lambda b, i: (b * nb + i, 0)
    width = MOBA_HEADS * LANES
    rows_spec = pl.BlockSpec((MOBA_BLOCK, width), row)
    cols_spec = pl.BlockSpec((1, width, MOBA_BLOCK), lambda b, i: (b, 0, i))
    rows_shape = jax.ShapeDtypeStruct((n_tok, width), BF16)
    cols_shape = jax.ShapeDtypeStruct((batch, width, nb * MOBA_BLOCK), BF16)
    return pl.pallas_call(
        functools.partial(_moba_prep_kernel, nb=nb),
        grid=(batch, nb),
        in_specs=[pl.BlockSpec((MOBA_BLOCK, MOBA_W), row)] * 2
                 + [pl.BlockSpec((1, MOBA_W, MOBA_BLOCK), lambda b, i: (b, 0, i)),
                    pl.BlockSpec((1, nb, MOBA_W), lambda b, i: (b, 0, 0))],
        out_specs=[cols_spec, rows_spec, pl.BlockSpec((1, MOBA_HEADS * V_SLOT_ROWS, MOBA_BLOCK), lambda b, i: (b, 0, i))],
        out_shape=[cols_shape, rows_shape,
                   jax.ShapeDtypeStruct((batch, MOBA_HEADS * V_SLOT_ROWS, nb * MOBA_BLOCK), BF16)],
        compiler_params=pltpu.CompilerParams(dimension_semantics=("arbitrary", "arbitrary"),
                                             vmem_limit_bytes=VMEM_LIMIT_BYTES),
        name="moba_prep",
    )(mq, mk, mvt, ksum)


def _moba_attn_kernel(gtab_ref, jtab_ref, kind_ref, qt_ref, k_ref, vt_ref, o_ref, m_sc, acc_sc, *, tile):
    p = pl.program_id(1)
    kind = kind_ref[p]

    def scores(h):
        sl = slice(h * LANES, (h + 1) * LANES)
        return jnp.dot(k_ref[:, sl], qt_ref[0, sl, :], preferred_element_type=F32)

    def step(own):
        if own:
            keep = (lax.broadcasted_iota(jnp.int32, (tile, tile), 0) <= lax.broadcasted_iota(jnp.int32, (tile, tile), 1))
        pending = [scores(h) for h in range(QK_LOOKAHEAD)]
        for h in range(MOBA_HEADS):
            s = pending.pop(0)
            if h + QK_LOOKAHEAD < MOBA_HEADS:
                pending.append(scores(h + QK_LOOKAHEAD))
            vsl = slice(h * V_SLOT_ROWS, (h + 1) * V_SLOT_ROWS)
            if own:
                s = jnp.where(keep, s, -jnp.inf)
                m_new = jnp.max(s, axis=0, keepdims=True)
            else:
                m_old = m_sc[h:h + 1, :]
                m_new = jnp.maximum(m_old, jnp.max(s, axis=0, keepdims=True))
            pexp = jnp.exp2(s - m_new)
            pv = jnp.dot(vt_ref[0, vsl, :], pexp.astype(BF16), preferred_element_type=F32)
            if own:
                acc_sc[vsl, :] = pv
            else:
                acc_sc[vsl, :] = jnp.exp2(m_old - m_new) * acc_sc[vsl, :] + pv
            m_sc[h:h + 1, :] = m_new

    @pl.when((kind & 3) == 0)
    def _():
        step(True)

    @pl.when((kind & 3) == 2)
    def _():
        step(False)

    @pl.when(kind >= 4)
    def _():
        outs = []
        for h in range(MOBA_HEADS):
            acc = acc_sc[h * V_SLOT_ROWS:(h + 1) * V_SLOT_ROWS, :]
            outs.append(acc[:MOBA_DH, :] / acc[MOBA_DH:MOBA_DH + 1, :])
        o_ref[...] = jnp.concatenate([jnp.concatenate(outs[2 * g:2 * g + 2], axis=0).T
                                      for g in range(MOBA_HEADS // 2)], axis=1)


def _moba_attn(q_ext_t, k_ext, v_ext_t, *, batch, nb):
    n_tok = k_ext.shape[0]
    assert nb % ATTN_TILE_BLOCKS == 0
    tile = ATTN_TILE_BLOCKS * MOBA_BLOCK
    n_tiles = nb // ATTN_TILE_BLOCKS
    gs, js, kinds = [], [], []
    for g in range(n_tiles):
        order = [(g, 0)] + [(j, 2) for j in range(g)]
        for idx, (j, kind) in enumerate(order):
            gs.append(g)
            js.append(j)
            kinds.append(kind + (4 if idx == len(order) - 1 else 0))
    tabs = [jnp.asarray(t, jnp.int32) for t in (gs, js, kinds)]
    width = MOBA_HEADS * LANES
    grid_spec = pltpu.PrefetchScalarGridSpec(
        num_scalar_prefetch=3,
        grid=(batch, len(gs)),
        in_specs=[pl.BlockSpec((1, width, tile), lambda b, p, gt, jt, kt: (b, 0, gt[p])),
                  pl.BlockSpec((tile, width), lambda b, p, gt, jt, kt: (b * n_tiles + jt[p], 0)),
                  pl.BlockSpec((1, MOBA_HEADS * V_SLOT_ROWS, tile), lambda b, p, gt, jt, kt: (b, 0, jt[p]))],
        out_specs=pl.BlockSpec((tile, MOBA_W), lambda b, p, gt, jt, kt: (b * n_tiles + gt[p], 0)),
        scratch_shapes=[pltpu.VMEM((MOBA_HEADS, tile), F32), pltpu.VMEM((MOBA_HEADS * V_SLOT_ROWS, tile), F32)],
    )
    return pl.pallas_call(
        functools.partial(_moba_attn_kernel, tile=tile),
        grid_spec=grid_spec,
        out_shape=jax.ShapeDtypeStruct((n_tok, MOBA_W), F32),
        compiler_params=pltpu.CompilerParams(dimension_semantics=("arbitrary", "arbitrary"),
                                             vmem_limit_bytes=VMEM_LIMIT_BYTES),
        name="moba_attn",
    )(*tabs, q_ext_t, k_ext, v_ext_t)


def _split3_bf16(x):
    x1 = x.astype(BF16)
    r1 = x - x1.astype(F32)
    x2 = r1.astype(BF16)
    x3 = (r1 - x2.astype(F32)).astype(BF16)
    return x1, x2, x3


def _gla_norm_gate(o, nw, gg):
    on = o * lax.rsqrt(jnp.mean(jnp.square(o), axis=-1, keepdims=True) + RMS_EPS) * nw
    return on * _silu(gg)


def _gla_kernel(q_ref, k_ref, lf_ref, v_ref, g_ref, nw_ref, o_ref, st_ref, st_sc, *, tt):
    t = pl.program_id(1)

    @pl.when(t == 0)
    def _():
        st_sc[...] = jnp.zeros(st_sc.shape, F32)

    c = GLA_CHUNK
    tril = lax.broadcasted_iota(jnp.int32, (c, c), 1) <= lax.broadcasted_iota(jnp.int32, (c, c), 0)
    ltri = jnp.where(tril, 1.0, 0.0).astype(BF16)
    nw = nw_ref[...]
    units = [(ci, h) for ci in range(tt // c) for h in range(GLA_HEADS)]
    where = {u: (slice(u[0] * c, (u[0] + 1) * c), slice(u[1] * LANES, (u[1] + 1) * LANES)) for u in units}

    cums = {}
    for u in units:
        g1, g2, g3 = _split3_bf16(lf_ref[where[u]])
        cums[u] = (jnp.dot(ltri, g1, preferred_element_type=F32) + jnp.dot(ltri, g2, preferred_element_type=F32)
                   + jnp.dot(ltri, g3, preferred_element_type=F32))
    qds, attn, upds, decays, intras = {}, {}, {}, {}, {}
    for u in units:
        b = cums[u]
        b_last = b[c - 1:c, :]
        q = q_ref[where[u]]
        k = k_ref[where[u]]
        qds[u] = (q * jnp.exp(b)).astype(BF16)
        kd = (k * jnp.exp(-b)).astype(BF16)
        attn[u] = lax.dot_general(qds[u], kd, NT_DIMS, preferred_element_type=F32)
        decays[u] = jnp.exp(b_last)
    for u in units:
        b = cums[u]
        kdec = (k_ref[where[u]] * jnp.exp(b[c - 1:c, :] - b)).astype(BF16)
        upds[u] = jnp.dot(v_ref[where[u]].T.astype(BF16), kdec, preferred_element_type=F32)
    for u in units:
        a = jnp.where(tril, attn[u], 0.0).astype(BF16)
        intras[u] = jnp.dot(a, v_ref[where[u]].astype(BF16), preferred_element_type=F32)
    state = [st_sc[h] for h in range(GLA_HEADS)]
    for u in units:
        h = u[1]
        o = lax.dot_general(qds[u], state[h].astype(BF16), NT_DIMS, preferred_element_type=F32) + intras[u]
        state[h] = state[h] * decays[u] + upds[u]
        o_ref[where[u]] = _gla_norm_gate(o, nw, g_ref[where[u]]).astype(o_ref.dtype)
    for h in range(GLA_HEADS):
        st_sc[h] = state[h]

    @pl.when(t == pl.num_programs(1) - 1)
    def _():
        for h in range(GLA_HEADS):
            st_ref[0, h] = st_sc[h].T[:GLA_DK, :]


def _gla_prompt(gq, gk, lf, gv, gg, norm_w, *, batch, seq, tt):
    n_tok = gq.shape[0]
    nt = seq // tt
    row = lambda b, t: (b * nt + t, 0)
    pad_w = GLA_HEADS * LANES
    return pl.pallas_call(
        functools.partial(_gla_kernel, tt=tt),
        grid=(batch, nt),
        in_specs=[pl.BlockSpec((tt, pad_w), row)] * 3 + [pl.BlockSpec((tt, GLA_V), row)] * 2
                 + [pl.BlockSpec((1, GLA_DV), lambda b, t: (0, 0))],
        out_specs=[pl.BlockSpec((tt, GLA_V), row),
                   pl.BlockSpec((1, GLA_HEADS, GLA_DK, GLA_DV), lambda b, t: (b, 0, 0, 0))],
        out_shape=[jax.ShapeDtypeStruct((n_tok, GLA_V), BF16),
                   jax.ShapeDtypeStruct((batch, GLA_HEADS, GLA_DK, GLA_DV), F32)],
        scratch_shapes=[pltpu.VMEM((GLA_HEADS, GLA_DV, LANES), F32)],
        compiler_params=pltpu.CompilerParams(dimension_semantics=("arbitrary", "arbitrary"),
                                             vmem_limit_bytes=VMEM_LIMIT_BYTES),
        name="gla_prompt",
    )(gq, gk, lf, gv, gg, norm_w)


def _gla_step_kernel(q_ref, k_ref, lf_ref, v_ref, g_ref, nw_ref, s_ref, o_ref, so_ref):
    eye = (lax.broadcasted_iota(jnp.int32, (GLA_DK, LANES), 0) == lax.broadcasted_iota(jnp.int32, (GLA_DK, LANES), 1))

    def column(rowvec):
        return jnp.sum(jnp.where(eye, rowvec, 0.0), axis=1, keepdims=True)

    nw = nw_ref[...]
    for h in range(GLA_HEADS):
        sl = slice(h * LANES, (h + 1) * LANES)
        decay = column(jnp.exp(lf_ref[0, :, sl]))
        kcol = column(k_ref[0, :, sl])
        qcol = column(q_ref[0, :, sl])
        s_new = decay * s_ref[0, h] + kcol * v_ref[0, :, sl]
        so_ref[0, h] = s_new
        o = jnp.sum(qcol * s_new, axis=0, keepdims=True)
        o_ref[0, :, sl] = _gla_norm_gate(o, nw, g_ref[0, :, sl]).astype(o_ref.dtype)


def _gla_step(gq, gk, lf, gv, gg, norm_w, state):
    n = gq.shape[0]
    pad_w = GLA_HEADS * LANES
    as3 = lambda a: a.reshape(n, 1, a.shape[-1])
    vec = lambda w: pl.BlockSpec((1, 1, w), lambda i: (i, 0, 0))
    st_spec = pl.BlockSpec((1, GLA_HEADS, GLA_DK, GLA_DV), lambda i: (i, 0, 0, 0))
    o, s_out = pl.pallas_call(
        _gla_step_kernel,
        grid=(n,),
        in_specs=[vec(pad_w)] * 3 + [vec(GLA_V)] * 2 + [pl.BlockSpec((1, GLA_DV), lambda i: (0, 0)), st_spec],
        out_specs=[vec(GLA_V), st_spec],
        out_shape=[jax.ShapeDtypeStruct((n, 1, GLA_V), BF16), jax.ShapeDtypeStruct(state.shape, F32)],
        compiler_params=pltpu.CompilerParams(dimension_semantics=("arbitrary",)),
        name="gla_step",
    )(as3(gq), as3(gk), as3(lf), as3(gv), as3(gg), norm_w, state)
    return o.reshape(n, GLA_V), s_out


ROUTE_PAGES_IN_FLIGHT = 16


def _sample_route_kernel(pt_ref, q_ref, ckt_hbm, top_ref, buf, sem, qb_sc, gp_sc,
                         *, layer, n_pages, page_size, nb_past):
    n = pl.program_id(0)
    n_seq = pl.num_programs(0)
    pages_per_block = MOBA_BLOCK // page_size
    blocks_in_flight = ROUTE_PAGES_IN_FLIGHT // pages_per_block

    def page_copy(seq, blk, a):
        slot = lax.rem(blk, blocks_in_flight) * pages_per_block + a
        phys = pt_ref[seq * n_pages + blk * pages_per_block + a]
        return pltpu.make_async_copy(ckt_hbm.at[layer, phys], buf.at[slot], sem.at[slot])

    @pl.when(n == 0)
    def _():
        for blk in range(blocks_in_flight):
            for a in range(pages_per_block):
                page_copy(n, jnp.int32(blk), a).start()

    eye = (lax.broadcasted_iota(jnp.int32, (MOBA_DH, MOBA_DH), 0)
           == lax.broadcasted_iota(jnp.int32, (MOBA_DH, MOBA_DH), 1))
    for h in range(MOBA_HEADS):
        qcol = jnp.sum(jnp.where(eye, q_ref[0, h:h + 1, :], 0.0), axis=1, keepdims=True)
        qb_sc[h] = jnp.broadcast_to(qcol, (MOBA_DH, page_size))

    def body(blk, carry):
        total = None
        for a in range(pages_per_block):
            page_copy(n, blk, a).wait()
            slot = lax.rem(blk, blocks_in_flight) * pages_per_block + a
            total = buf[slot] if total is None else total + buf[slot]
        gp_sc[blk] = jnp.sum(total * qb_sc[...], axis=1)
        nxt = blk + blocks_in_flight

        @pl.when(nxt < nb_past)
        def _():
            for a in range(pages_per_block):
                page_copy(n, nxt, a).start()

        @pl.when((nxt >= nb_past) & (n + 1 < n_seq))
        def _():
            for a in range(pages_per_block):
                page_copy(n + 1, nxt - nb_past, a).start()
        return carry

    lax.fori_loop(0, nb_past, body, 0)

    gate = jnp.sum(gp_sc[...], axis=-1, keepdims=True) * (1.0 / MOBA_BLOCK)
    jidx = lax.broadcasted_iota(jnp.int32, gate.shape, 0).astype(F32)
    lane = _lane_iota((MOBA_HEADS, LANES))
    out = jnp.zeros((MOBA_HEADS, LANES), jnp.int32)
    cur = gate
    for r in range(min(MOBA_TOPK, nb_past)):
        m = jnp.max(cur, axis=0, keepdims=True)
        idx = jnp.min(jnp.where(cur == m, jidx, 1e9), axis=0, keepdims=True)
        cur = jnp.where(jidx == idx, -jnp.inf, cur)
        out = jnp.where(lane == r, idx[0].astype(jnp.int32), out)
    top_ref[0] = out


def _sample_route(page_table_flat, q_heads, cache_kt, *, layer, nb_past):
    n = q_heads.shape[0]
    page_size = cache_kt.shape[4]
    n_pages = page_table_flat.shape[0] // n
    pages_per_block = MOBA_BLOCK // page_size
    assert nb_past % (ROUTE_PAGES_IN_FLIGHT // pages_per_block) == 0
    grid_spec = pltpu.PrefetchScalarGridSpec(
        num_scalar_prefetch=1,
        grid=(n,),
        in_specs=[pl.BlockSpec((1, MOBA_HEADS, MOBA_DH), lambda i, pt: (i, 0, 0)),
                  pl.BlockSpec(memory_space=pl.ANY)],
        out_specs=pl.BlockSpec((1, MOBA_HEADS, LANES), lambda i, pt: (i, 0, 0)),
        scratch_shapes=[pltpu.VMEM((ROUTE_PAGES_IN_FLIGHT, MOBA_HEADS, MOBA_DH, page_size), F32),
                        pltpu.SemaphoreType.DMA((ROUTE_PAGES_IN_FLIGHT,)),
                        pltpu.VMEM((MOBA_HEADS, MOBA_DH, page_size), F32),
                        pltpu.VMEM((nb_past, MOBA_HEADS, page_size), F32)],
    )
    return pl.pallas_call(
        functools.partial(_sample_route_kernel, layer=layer, n_pages=n_pages, page_size=page_size, nb_past=nb_past),
        grid_spec=grid_spec,
        out_shape=jax.ShapeDtypeStruct((n, MOBA_HEADS, LANES), jnp.int32),
        compiler_params=pltpu.CompilerParams(dimension_semantics=("arbitrary",)),
        name="sample_route",
    )(page_table_flat, q_heads, cache_kt)


def _sample_attn_kernel(top_ref, pt_ref, q_ref, kn_ref, vn_ref, ckt_hbm, cvt_hbm, o_ref, kbuf, vbuf, sem,
                        *, layer, n_pages, page_size, n_sel):
    n = pl.program_id(0)
    n_seq = pl.num_programs(0)
    pages_per_block = MOBA_BLOCK // page_size

    def copies(seq, h):
        half = lax.rem(seq, 2)
        out = []
        for r in range(n_sel):
            blk = top_ref[(seq * MOBA_HEADS + h) * n_sel + r]
            for a in range(pages_per_block):
                phys = pt_ref[seq * n_pages + blk * pages_per_block + a]
                dst = pl.ds((r * pages_per_block + a) * page_size, page_size)
                out.append(pltpu.make_async_copy(ckt_hbm.at[layer, phys, h], kbuf.at[half, h, :, dst],
                                                 sem.at[half, 0, h]))
                out.append(pltpu.make_async_copy(cvt_hbm.at[layer, phys, h], vbuf.at[half, h, :, dst],
                                                 sem.at[half, 1, h]))
        return out

    @pl.when(n == 0)
    def _():
        for h in range(MOBA_HEADS):
            for cp in copies(n, h):
                cp.start()

    @pl.when(n + 1 < n_seq)
    def _():
        for h in range(MOBA_HEADS):
            for cp in copies(n + 1, h):
                cp.start()

    half = lax.rem(n, 2)
    for h in range(MOBA_HEADS):
        for cp in copies(n, h):
            cp.wait()
    qs = [q_ref[0, h:h + 1, :] for h in range(MOBA_HEADS)]
    scores = [jnp.dot(jnp.broadcast_to(qs[h], (SUBLANES, MOBA_DH)).astype(BF16), kbuf[half, h].astype(BF16),
                      preferred_element_type=F32) for h in range(MOBA_HEADS)]
    probs, p_owns, denoms = [], [], []
    for h in range(MOBA_HEADS):
        s_own = jnp.sum(qs[h] * kn_ref[0, h:h + 1, :], axis=-1, keepdims=True)
        m = jnp.maximum(jnp.max(scores[h], axis=-1, keepdims=True), s_own)
        p = jnp.exp(scores[h] - m)
        p_own = jnp.exp(s_own - m)
        probs.append(p.astype(BF16))
        p_owns.append(p_own)
        denoms.append(jnp.sum(p, axis=-1, keepdims=True) + p_own)
    outs = [lax.dot_general(probs[h], vbuf[half, h].astype(BF16), NT_DIMS, preferred_element_type=F32)
            for h in range(MOBA_HEADS)]
    for h in range(MOBA_HEADS):
        o = outs[h] + p_owns[h] * vn_ref[0, h:h + 1, :]
        o_ref[0, h:h + 1, :] = (o / denoms[h])[0:1, :]


def _sample_attn(top_flat, page_table_flat, q_heads, k_new, v_new, cache_kt, cache_vt, *, layer, n_sel):
    n = q_heads.shape[0]
    page_size = cache_kt.shape[4]
    n_pages = page_table_flat.shape[0] // n
    rows = n_sel * MOBA_BLOCK
    head_spec = pl.BlockSpec((1, MOBA_HEADS, MOBA_DH), lambda i, tp, pt: (i, 0, 0))
    grid_spec = pltpu.PrefetchScalarGridSpec(
        num_scalar_prefetch=2,
        grid=(n,),
        in_specs=[head_spec, head_spec, head_spec,
                  pl.BlockSpec(memory_space=pl.ANY), pl.BlockSpec(memory_space=pl.ANY)],
        out_specs=head_spec,
        scratch_shapes=[pltpu.VMEM((2, MOBA_HEADS, MOBA_DH, rows), F32), pltpu.VMEM((2, MOBA_HEADS, MOBA_DH, rows), F32),
                        pltpu.SemaphoreType.DMA((2, 2, MOBA_HEADS))],
    )
    return pl.pallas_call(
        functools.partial(_sample_attn_kernel, layer=layer, n_pages=n_pages, page_size=page_size, n_sel=n_sel),
        grid_spec=grid_spec,
        out_shape=jax.ShapeDtypeStruct((n, MOBA_HEADS, MOBA_DH), F32),
        compiler_params=pltpu.CompilerParams(dimension_semantics=("arbitrary",)),
        name="sample_attn",
    )(top_flat, page_table_flat, q_heads, k_new, v_new, cache_kt, cache_vt)


def _merge_kernel(x_ref, go_ref, mo_ref, mg_ref, w1_ref, w2_ref, g_ref, b_ref, y_ref, *, alpha):
    mo = (mo_ref[...] * _silu(mg_ref[...])).astype(BF16)
    z = (alpha * x_ref[...] + jnp.dot(go_ref[...], w1_ref[...], preferred_element_type=F32)
         + jnp.dot(mo, w2_ref[...], preferred_element_type=F32))
    mu = jnp.mean(z, axis=-1, keepdims=True)
    zc = z - mu
    var = jnp.mean(jnp.square(zc), axis=-1, keepdims=True)
    y_ref[...] = zc * lax.rsqrt(var + LN_EPS) * g_ref[...] + b_ref[...]


def _merge(x2d, gla_o, moba_o, mg, w1, w2, ln_g, ln_b, *, tm, alpha):
    n_tok, d_model = x2d.shape
    row = lambda i: (i, 0)
    const = lambda i: (0, 0)
    return pl.pallas_call(
        functools.partial(_merge_kernel, alpha=alpha),
        grid=(n_tok // tm,),
        in_specs=[pl.BlockSpec((tm, d_model), row), pl.BlockSpec((tm, GLA_V), row), pl.BlockSpec((tm, MOBA_W), row),
                  pl.BlockSpec((tm, MOBA_W), row), pl.BlockSpec(w1.shape, const), pl.BlockSpec(w2.shape, const),
                  pl.BlockSpec((1, d_model), const), pl.BlockSpec((1, d_model), const)],
        out_specs=pl.BlockSpec((tm, d_model), row),
        out_shape=jax.ShapeDtypeStruct((n_tok, d_model), F32),
        compiler_params=pltpu.CompilerParams(dimension_semantics=("arbitrary",), vmem_limit_bytes=VMEM_LIMIT_BYTES),
        name="merge",
    )(x2d, gla_o, moba_o, mg, w1, w2, ln_g, ln_b)


def _split_w_in(w_in_l, w_gk2_l, b_gk_l):
    n_gla = 2 * GLA_QK + 2 * GLA_V
    wg = w_in_l[:, :n_gla].astype(BF16)
    wl = jnp.pad(w_in_l[:, n_gla:n_gla + GLA_GATE_RANK], ((0, 0), (0, LANES - GLA_GATE_RANK))).astype(BF16)
    wm = w_in_l[:, n_gla + GLA_GATE_RANK:].astype(BF16)
    wgk2 = jnp.pad(w_gk2_l, ((0, LANES - GLA_GATE_RANK), (0, 0))).astype(BF16)
    return wg, wl, wgk2, b_gk_l.reshape(1, GLA_QK), wm


def kernel(x_prompt, x_sample, cache_k, cache_v, state_gla, page_table, w_in, w_gk2, b_gk, gla_norm_w, w_out,
           ln_g, ln_b):
    batch, seq, d_model = x_prompt.shape
    dec_batch, dec_seq, _ = x_sample.shape
    depth = w_in.shape[0]
    page_size = cache_k.shape[2]
    n_pages = page_table.shape[1]
    past_len = n_pages * page_size
    assert dec_seq == 1 and past_len % MOBA_BLOCK == 0 and MOBA_BLOCK % page_size == 0
    assert seq % MOBA_BLOCK == 0
    nb = seq // MOBA_BLOCK
    assert nb >= MOBA_TOPK and FLAG_LANE0 + nb <= LANES
    nb_past = past_len // MOBA_BLOCK
    n_sel = min(MOBA_TOPK, nb_past)
    alpha = (2.0 * depth) ** 0.25

    half = MOBA_DH // 2
    inv = ROPE_THETA ** (-jnp.arange(half, dtype=F32) / half)
    inv_lane = jnp.tile(inv, LANES // half).reshape(1, LANES)
    pt_flat = page_table.reshape(-1)
    cache_kt = jnp.transpose(cache_k, (0, 1, 3, 4, 2))
    cache_vt = jnp.transpose(cache_v, (0, 1, 3, 4, 2))

    xp = x_prompt.reshape(batch * seq, d_model)
    xs = x_sample.reshape(dec_batch * dec_seq, d_model)
    kp_l, vp_l, sp_l, ks_l, vs_l, ss_l = [], [], [], [], [], []
    for l in range(depth):
        w_parts = _split_w_in(w_in[l], w_gk2[l], b_gk[l])
        norm_w = gla_norm_w[l].reshape(1, GLA_DV)
        w1 = w_out[l, :GLA_V].astype(BF16)
        w2 = w_out[l, GLA_V:].astype(BF16)
        g_row = ln_g[l].reshape(1, d_model)
        b_row = ln_b[l].reshape(1, d_model)

        gq, gk, lf, gv, gg, mq, mk, mg, mkt, mvt, ksum = _project(
            xp, w_parts, inv_lane, tm=MOBA_BLOCK, seq_len=seq, pos_base=0, block_sums=True)
        ksum = ksum.reshape(batch, nb, SUBLANES, MOBA_W)[:, :, 0, :]
        q_ext_t, k_ext, v_ext_t = _moba_prep(mq, mk, mvt, ksum, batch=batch, nb=nb)
        moba_p = _moba_attn(q_ext_t, k_ext, v_ext_t, batch=batch, nb=nb)
        gla_p, st_p = _gla_prompt(gq, gk, lf, gv, gg, norm_w, batch=batch, seq=seq, tt=GLA_TOKENS_PER_STEP)
        kp_l.append(jnp.transpose(mkt.reshape(batch, MOBA_HEADS, MOBA_DH, seq), (0, 3, 1, 2)))
        vp_l.append(jnp.transpose(mvt.reshape(batch, MOBA_HEADS, MOBA_DH, seq), (0, 3, 1, 2)))
        sp_l.append(st_p)

        sq, sk, slf, sv, sg, smq, smk, smg, smkt, smvt, smv = _project(
            xs, w_parts, inv_lane, tm=dec_batch, seq_len=dec_seq, pos_base=past_len, block_sums=False)
        gla_s, st_s = _gla_step(sq, sk, slf, sv, sg, norm_w, state_gla[l])
        q_heads = smq.reshape(dec_batch, MOBA_HEADS, MOBA_DH)
        k_new = smk.reshape(dec_batch, MOBA_HEADS, MOBA_DH)
        v_new = smv.reshape(dec_batch, MOBA_HEADS, MOBA_DH)
        top = _sample_route(pt_flat, q_heads, cache_kt, layer=l, nb_past=nb_past)
        top_flat = top[:, :, :n_sel].reshape(-1)
        moba_s = _sample_attn(top_flat, pt_flat, q_heads, k_new, v_new, cache_kt, cache_vt, layer=l, n_sel=n_sel)
        moba_s = moba_s.reshape(dec_batch, MOBA_W)
        ks_l.append(jnp.transpose(smkt.reshape(MOBA_HEADS, MOBA_DH, dec_batch), (2, 0, 1))[:, None])
        vs_l.append(jnp.transpose(smvt.reshape(MOBA_HEADS, MOBA_DH, dec_batch), (2, 0, 1))[:, None])
        ss_l.append(st_s)

        xp = _merge(xp, gla_p, moba_p, mg, w1, w2, g_row, b_row, tm=2 * MOBA_BLOCK, alpha=alpha)
        xs = _merge(xs, gla_s, moba_s, smg, w1, w2, g_row, b_row, tm=dec_batch, alpha=alpha)

    return (xp.reshape(batch, seq, d_model), xs.reshape(dec_batch, dec_seq, d_model),
            jnp.stack(kp_l), jnp.stack(vp_l), jnp.stack(sp_l), jnp.stack(ks_l), jnp.stack(vs_l), jnp.stack(ss_l))
```
